```python
import math
import jax, jax.numpy as jnp
from jax import lax
import numpy as np

D_MODEL = 1024
BATCH = 2
SEQ = 16384
DEPTH = 4

GRID_W = 64
CTX_LEN = 256
HEAD_DIM = 64
GQA_Q_HEADS = 8
GQA_KV_HEADS = 2
GQA_GROUP = GQA_Q_HEADS // GQA_KV_HEADS
DIFF_HEADS = 4
DIFF_V_DIM = 2 * HEAD_DIM
CONV_CH = 512
CONV_WIDTH = 31
D_FF = -(-8 * D_MODEL // (3 * 256)) * 256
Q_BLOCK = 128
ROPE_THETA = 10000.0
EPS = 1e-6
ATTN_SCALE = HEAD_DIM ** -0.5

GQA_Q_W = GQA_Q_HEADS * HEAD_DIM
GQA_KV_W = GQA_KV_HEADS * HEAD_DIM
DIFF_QK_W = DIFF_HEADS * 2 * HEAD_DIM
DIFF_V_W = DIFF_HEADS * DIFF_V_DIM
CONV_IN_W = 2 * CONV_CH
IN_WIDTHS = (GQA_Q_W, GQA_KV_W, GQA_KV_W, DIFF_QK_W, DIFF_QK_W, DIFF_V_W, CONV_IN_W)
D_IN = sum(IN_WIDTHS)
IN_OFFSETS = tuple(int(v) for v in np.cumsum(IN_WIDTHS)[:-1])
N_BRANCH = 3

kernel_name = "hybrid_gqa_diffattn_conformer_prefix_dit"


def rms_norm(x, g):
    xf = x.astype(jnp.float32)
    y = xf * lax.rsqrt(jnp.mean(xf * xf, axis=-1, keepdims=True) + EPS)
    return (y * g.astype(jnp.float32)).astype(x.dtype)


def layer_norm(x, g, b):
    xf = x.astype(jnp.float32)
    mu = jnp.mean(xf, axis=-1, keepdims=True)
    xc = xf - mu
    var = jnp.mean(xc * xc, axis=-1, keepdims=True)
    y = xc * lax.rsqrt(var + EPS) * g.astype(jnp.float32) + b.astype(jnp.float32)
    return y.astype(x.dtype)


def modulate(h, shift, scale):
    return h * (1 + scale) + shift


def axial_rope_tables(n_tokens):
    rows = n_tokens // GRID_W
    row = jnp.repeat(jnp.arange(rows, dtype=jnp.float32), GRID_W)
    col = jnp.tile(jnp.arange(GRID_W, dtype=jnp.float32), rows)
    n_freq = HEAD_DIM // 4
    inv = ROPE_THETA ** (-jnp.arange(n_freq, dtype=jnp.float32) / n_freq)
    ang = jnp.stack([row[:, None] * inv, col[:, None] * inv], axis=1)
    return jnp.cos(ang), jnp.sin(ang)


def apply_axial_rope(x, cos, sin):
    n_freq = HEAD_DIM // 4
    mid = (1,) * (x.ndim - 3)
    cb = cos.reshape((cos.shape[0],) + mid + (2, n_freq))
    sb = sin.reshape((sin.shape[0],) + mid + (2, n_freq))
    xf = x.astype(jnp.float32).reshape(x.shape[:-1] + (2, 2, n_freq))
    x1 = xf[..., 0, :]
    x2 = xf[..., 1, :]
    out = jnp.stack([x1 * cb - x2 * sb, x2 * cb + x1 * sb], axis=-2)
    return out.reshape(x.shape).astype(x.dtype)


def project_heads(p, g_q_gqa, g_k_gqa, g_q_diff, g_k_diff, rope):
    B, T, _ = p.shape
    gq, gk, gv, dq, dk, dv, cu = jnp.split(p, IN_OFFSETS, axis=-1)
    gq = rms_norm(gq.reshape(B, T, GQA_KV_HEADS, GQA_GROUP, HEAD_DIM), g_q_gqa)
    gk = rms_norm(gk.reshape(B, T, GQA_KV_HEADS, HEAD_DIM), g_k_gqa)
    gv = gv.reshape(B, T, GQA_KV_HEADS, HEAD_DIM)
    dq = rms_norm(dq.reshape(B, T, DIFF_HEADS, 2, HEAD_DIM), g_q_diff)
    dk = rms_norm(dk.reshape(B, T, DIFF_HEADS, 2, HEAD_DIM), g_k_diff)
    dv = dv.reshape(B, T, DIFF_HEADS, DIFF_V_DIM)
    if rope is not None:
        cos, sin = rope
        gq = apply_axial_rope(gq, cos, sin)
        gk = apply_axial_rope(gk, cos, sin)
        dq = apply_axial_rope(dq, cos, sin)
        dk = apply_axial_rope(dk, cos, sin)
    return gq, gk, gv, dq, dk, dv, cu


def gqa_attend(q, k, v):
    s = jnp.einsum('bqkgd,btkd->bkgqt', q * ATTN_SCALE, k).astype(jnp.float32)
    p = jax.nn.softmax(s, axis=-1).astype(v.dtype)
    return jnp.einsum('bkgqt,btkd->bqkgd', p, v)


def diff_attend(q, k, v, lam):
    s = jnp.einsum('bqhmd,bthmd->bhmqt', q * ATTN_SCALE, k).astype(jnp.float32)
    p = jax.nn.softmax(s, axis=-1)
    w = (p[:, :, 0] - lam * p[:, :, 1]).astype(v.dtype)
    return jnp.einsum('bhqt,bthe->bqhe', w, v)


def blocked_queries(fn, q):
    B, T = q.shape[:2]
    nb = T // Q_BLOCK
    qb = jnp.moveaxis(q.reshape((B, nb, Q_BLOCK) + q.shape[2:]), 1, 0)
    ob = lax.map(fn, qb)
    return jnp.moveaxis(ob, 0, 1).reshape((B, T) + ob.shape[3:])


def depthwise_conv(u, w, b):
    out = lax.conv_general_dilated(
        u, w[:, None, :].astype(u.dtype), window_strides=(1,),
        padding=[(CONV_WIDTH // 2, CONV_WIDTH // 2)],
        dimension_numbers=('NWC', 'WIO', 'NWC'),
        feature_group_count=u.shape[-1])
    return out + b


def conformer_conv(cu, w_dw, b_dw, g_ln, b_ln, w_c):
    a, g = jnp.split(cu, 2, axis=-1)
    u = a * jax.nn.sigmoid(g)
    u = depthwise_conv(u, w_dw, b_dw)
    u = jax.nn.silu(layer_norm(u, g_ln, b_ln))
    return u @ w_c


def merge_branches(h, o_gqa, o_diff, cu, lam_init, g_subln, w_a, w_b, w_dw, b_dw, g_ln, b_ln, w_c,
                   w_gate, b_gate, w_out):
    B, T, _ = h.shape
    y_a = o_gqa.reshape(B, T, GQA_Q_W) @ w_a
    o_diff = rms_norm(o_diff, g_subln) * (1.0 - lam_init)
    y_b = o_diff.reshape(B, T, DIFF_V_W) @ w_b
    y_c = conformer_conv(cu, w_dw, b_dw, g_ln, b_ln, w_c)
    gates = jax.nn.sigmoid(h @ w_gate + b_gate)
    g_a, g_b, g_c = jnp.split(gates, N_BRANCH, axis=-1)
    return (g_a * y_a + g_b * y_b + g_c * y_c) @ w_out


def swiglu(h, w_in, w_out):
    g, u = jnp.split(h @ w_in, 2, axis=-1)
    return (jax.nn.silu(g) * u) @ w_out


def setup_inputs(seed: int = 0) -> dict:
    key = jax.random.key(seed)
    ks = jax.random.split(key, 32)
    f32 = jnp.float32
    nrm = lambda k, shape, s: jax.random.normal(k, shape, f32) * s
    gain = lambda k, shape: 1.0 + 0.02 * jax.random.normal(k, shape, f32)
    L = DEPTH
    return {
        "x": nrm(ks[0], (BATCH, SEQ, D_MODEL), 1.0),
        "c": nrm(ks[1], (BATCH, D_MODEL), 1.0),
        "ctx": nrm(ks[2], (BATCH, CTX_LEN, D_MODEL), 1.0),
        "c_ctx": nrm(ks[3], (D_MODEL,), 1.0),
        "w_mod": nrm(ks[4], (L, D_MODEL, 6 * D_MODEL), 0.5 * D_MODEL ** -0.5),
        "b_mod": nrm(ks[5], (L, 6 * D_MODEL), 0.02),
        "g_norm1": gain(ks[6], (L, D_MODEL)),
        "w_in": nrm(ks[7], (L, D_MODEL, D_IN), D_MODEL ** -0.5),
        "w_gate": nrm(ks[8], (L, D_MODEL, N_BRANCH * D_MODEL), D_MODEL ** -0.5),
        "b_gate": nrm(ks[9], (L, N_BRANCH * D_MODEL), 0.02),
        "g_q_gqa": gain(ks[10], (L, HEAD_DIM)),
        "g_k_gqa": gain(ks[11], (L, HEAD_DIM)),
        "g_q_diff": gain(ks[12], (L, HEAD_DIM)),
        "g_k_diff": gain(ks[13], (L, HEAD_DIM)),
        "lambda_q1": nrm(ks[14], (L, HEAD_DIM), 0.1),
        "lambda_k1": nrm(ks[15], (L, HEAD_DIM), 0.1),
        "lambda_q2": nrm(ks[16], (L, HEAD_DIM), 0.1),
        "lambda_k2": nrm(ks[17], (L, HEAD_DIM), 0.1),
        "g_subln": gain(ks[18], (L, DIFF_V_DIM)),
        "w_dw": nrm(ks[19], (L, CONV_WIDTH, CONV_CH), CONV_WIDTH ** -0.5),
        "b_dw": nrm(ks[20], (L, CONV_CH), 0.02),
        "g_conv_ln": gain(ks[21], (L, CONV_CH)),
        "b_conv_ln": nrm(ks[22], (L, CONV_CH), 0.02),
        "w_a": nrm(ks[23], (L, GQA_Q_W, D_MODEL), GQA_Q_W ** -0.5),
        "w_b": nrm(ks[24], (L, DIFF_V_W, D_MODEL), DIFF_V_W ** -0.5),
        "w_c": nrm(ks[25], (L, CONV_CH, D_MODEL), CONV_CH ** -0.5),
        "w_out": nrm(ks[26], (L, D_MODEL, D_MODEL), D_MODEL ** -0.5),
        "g_norm2": gain(ks[27], (L, D_MODEL)),
        "w_ffn_in": nrm(ks[28], (L, D_MODEL, 2 * D_FF), D_MODEL ** -0.5),
        "w_ffn_out": nrm(ks[29], (L, D_FF, D_MODEL), D_FF ** -0.5),
    }


def reference(x, c, ctx, c_ctx, w_mod, b_mod, g_norm1, w_in, w_gate, b_gate,
              g_q_gqa, g_k_gqa, g_q_diff, g_k_diff,
              lambda_q1, lambda_k1, lambda_q2, lambda_k2, g_subln,
              w_dw, b_dw, g_conv_ln, b_conv_ln, w_a, w_b, w_c, w_out,
              g_norm2, w_ffn_in, w_ffn_out):
    n_lat = x.shape[1]
    rope = axial_rope_tables(n_lat)
    xc = ctx
    for l in range(DEPTH):
        last = l == DEPTH - 1
        lam_init = 0.8 - 0.6 * math.exp(-0.3 * l)
        lam = (jnp.exp(jnp.sum(lambda_q1[l].astype(jnp.float32) * lambda_k1[l].astype(jnp.float32)))
               - jnp.exp(jnp.sum(lambda_q2[l].astype(jnp.float32) * lambda_k2[l].astype(jnp.float32)))
               + lam_init)
        mod = jax.nn.silu(c) @ w_mod[l] + b_mod[l]
        sh1, sc1, gt1, sh2, sc2, gt2 = jnp.split(mod[:, None, :], 6, axis=-1)
        mod_c = jax.nn.silu(c_ctx) @ w_mod[l] + b_mod[l]
        csh1, csc1, cgt1, csh2, csc2, cgt2 = jnp.split(mod_c, 6, axis=-1)
        qk_gains = (g_q_gqa[l], g_k_gqa[l], g_q_diff[l], g_k_diff[l])
        branch_w = (g_subln[l], w_a[l], w_b[l], w_dw[l], b_dw[l], g_conv_ln[l], b_conv_ln[l], w_c[l],
                    w_gate[l], b_gate[l], w_out[l])

        h = modulate(rms_norm(x, g_norm1[l]), sh1, sc1)
        hc = modulate(rms_norm(xc, g_norm1[l]), csh1, csc1)
        gq, gk, gv, dq, dk, dv, cu = project_heads(h @ w_in[l], *qk_gains, rope)
        gqc, gkc, gvc, dqc, dkc, dvc, cuc = project_heads(hc @ w_in[l], *qk_gains, None)
        gk_all = jnp.concatenate([gkc, gk], axis=1)
        gv_all = jnp.concatenate([gvc, gv], axis=1)
        dk_all = jnp.concatenate([dkc, dk], axis=1)
        dv_all = jnp.concatenate([dvc, dv], axis=1)
        o_gqa = blocked_queries(lambda qb: gqa_attend(qb, gk_all, gv_all), gq)
        o_diff = blocked_queries(lambda qb: diff_attend(qb, dk_all, dv_all, lam), dq)
        y = merge_branches(h, o_gqa, o_diff, cu, lam_init, *branch_w)
        if not last:
            oc_gqa = gqa_attend(gqc, gkc, gvc)
            oc_diff = diff_attend(dqc, dkc, dvc, lam)
            yc = merge_branches(hc, oc_gqa, oc_diff, cuc, lam_init, *branch_w)
        x = x + gt1 * y
        if not last:
            xc = xc + cgt1 * yc

        x = x + gt2 * swiglu(modulate(rms_norm(x, g_norm2[l]), sh2, sc2), w_ffn_in[l], w_ffn_out[l])
        if not last:
            xc = xc + cgt2 * swiglu(modulate(rms_norm(xc, g_norm2[l]), csh2, csc2),
                                    w_ffn_in[l], w_ffn_out[l])
    return x
```

```python
import functools
import math

import jax
import jax.numpy as jnp
from jax import lax
from jax.experimental import pallas as pl
from jax.experimental.pallas import tpu as pltpu

F32 = jnp.float32
BF16 = jnp.bfloat16

D_MODEL = 1024
HEAD_DIM = 64
GQA_Q_HEADS = 8
GQA_KV_HEADS = 2
GQA_GROUP = GQA_Q_HEADS // GQA_KV_HEADS
DIFF_HEADS = 4
DIFF_V_DIM = 2 * HEAD_DIM
CONV_CH = 512
CONV_WIDTH = 31
CONV_HALF = CONV_WIDTH // 2
D_FF = 2816
GRID_W = 64
ROPE_THETA = 10000.0
EPS = 1e-6
ATTN_SCALE = HEAD_DIM ** -0.5
N_BRANCH = 3

GQA_Q_W = GQA_Q_HEADS * HEAD_DIM
GQA_KV_W = GQA_KV_HEADS * HEAD_DIM
DIFF_QK_W = DIFF_HEADS * 2 * HEAD_DIM
DIFF_V_W = DIFF_HEADS * DIFF_V_DIM
CONV_IN_W = 2 * CONV_CH
OFF_GQ = 0
OFF_GK = OFF_GQ + GQA_Q_W
OFF_GV = OFF_GK + GQA_KV_W
OFF_DQ = OFF_GV + GQA_KV_W
OFF_DK = OFF_DQ + DIFF_QK_W
OFF_DV = OFF_DK + DIFF_QK_W
OFF_CU = OFF_DV + DIFF_V_W
D_IN = OFF_CU + CONV_IN_W
HEADS_W = OFF_CU

TILE = 256
HALO = 16
MOD_ROWS = 16
MOD_BLOCK_N = 1536
NEG_BIG = -1e30
VMEM_LIMIT = 52 * 1024 * 1024


def _cparams(n_axes):
    return pltpu.CompilerParams(
        dimension_semantics=("arbitrary",) * n_axes, vmem_limit_bytes=VMEM_LIMIT)


def _const_spec(shape):
    nd = len(shape)
    return pl.BlockSpec(shape, lambda *_: (0,) * nd, pipeline_mode=pl.Buffered(1))


def _dot(a, b):
    return jnp.dot(a, b, preferred_element_type=F32)


def _split_bf16(a):
    hi = a.astype(BF16)
    lo = (a - hi.astype(F32)).astype(BF16)
    return hi, lo


def _norm_mod(x, g, shift, scale):
    ms = jnp.mean(x * x, axis=-1, keepdims=True)
    y = x * lax.rsqrt(ms + EPS) * g
    return y * (1.0 + scale) + shift


def _mod_kernel(c_ref, w_ref, b_ref, o_ref):
    c = c_ref[...]
    a = c * jax.nn.sigmoid(c)
    a_hi, a_lo = _split_bf16(a)
    w_hi, w_lo = _split_bf16(w_ref[...])
    o_ref[...] = _dot(a_hi, w_hi) + _dot(a_hi, w_lo) + _dot(a_lo, w_hi) + b_ref[...]


def _modulation(cvec, w_mod, b_mod):
    depth, d, n = w_mod.shape
    return pl.pallas_call(
        _mod_kernel,
        grid=(depth, n // MOD_BLOCK_N),
        in_specs=[
            pl.BlockSpec((MOD_ROWS, d), lambda l, j: (0, 0)),
            pl.BlockSpec((None, d, MOD_BLOCK_N), lambda l, j: (l, 0, j)),
            pl.BlockSpec((None, 1, MOD_BLOCK_N), lambda l, j: (l, 0, j)),
        ],
        out_specs=pl.BlockSpec((None, MOD_ROWS, MOD_BLOCK_N), lambda l, j: (l, 0, j)),
        out_shape=jax.ShapeDtypeStruct((depth, MOD_ROWS, n), F32),
        compiler_params=_cparams(2),
        name="modulation",
    )(cvec, w_mod, b_mod.reshape(depth, 1, n))


def _proj_kernel(x_ref, mod_ref, g1_ref, wt_ref, wcu_ref, gains_ref, cos_ref, sin_ref,
                 qt_ref, k_ref, vt_ref, dqt_ref, dk_ref, dvt_ref, u_ref):
    hb = _norm_mod(x_ref[...], g1_ref[...], mod_ref[0:1, :], mod_ref[1:2, :]).astype(BF16)
    pt = lax.dot_general(wt_ref[...], hb, (((1,), (1,)), ((), ())), preferred_element_type=F32)
    cos = cos_ref[...][None]
    sin = sin_ref[...][None]

    def norm_rope(t, gain_idx, scale):
        n = t.shape[0] // HEAD_DIM
        t3 = t.reshape(n, HEAD_DIM, TILE)
        ms = jnp.mean(t3 * t3, axis=1, keepdims=True)
        y = t3 * lax.rsqrt(ms + EPS) * gains_ref[gain_idx][None]
        q = HEAD_DIM // 4
        swapped = jnp.concatenate(
            [y[:, q:2 * q], y[:, 0:q], y[:, 3 * q:4 * q], y[:, 2 * q:3 * q]], axis=1)
        y = y * cos + swapped * sin
        if scale != 1.0:
            y = y * scale
        return y.reshape(t.shape)

    qt_ref[...] = norm_rope(pt[OFF_GQ:OFF_GK], 0, ATTN_SCALE).astype(BF16)
    k_ref[...] = norm_rope(pt[OFF_GK:OFF_GV], 1, 1.0).T.astype(BF16)
    vt_ref[...] = pt[OFF_GV:OFF_DQ].astype(BF16)
    dqt_ref[...] = norm_rope(pt[OFF_DQ:OFF_DK], 2, ATTN_SCALE).astype(BF16)
    dk_ref[...] = norm_rope(pt[OFF_DK:OFF_DV], 3, 1.0).T.astype(BF16)
    dvt_ref[...] = pt[OFF_DV:OFF_CU].astype(BF16)
    cu = _dot(hb, wcu_ref[...])
    u_ref[...] = cu[:, :CONV_CH] * jax.nn.sigmoid(cu[:, CONV_CH:])


def _project(xs, modsel, g1, wt, wcu, gains, cos_t, sin_t):
    b, t, d = xs.shape
    nt = t // TILE
    tile_map = lambda bi, i: (bi, i, 0)
    chunk_map = lambda bi, i: (bi, i, 0, 0)
    lane_map = lambda bi, i: (bi, 0, i)
    return pl.pallas_call(
        _proj_kernel,
        grid=(b, nt),
        in_specs=[
            pl.BlockSpec((None, TILE, d), tile_map),
            pl.BlockSpec((None, None, 6, d), lambda bi, i: (bi, jnp.minimum(i, 1), 0, 0)),
            _const_spec((1, d)),
            _const_spec((HEADS_W, d)),
            _const_spec((d, CONV_IN_W)),
            _const_spec((4, HEAD_DIM, TILE)),
            pl.BlockSpec((HEAD_DIM, TILE), lambda bi, i: (0, i)),
            pl.BlockSpec((HEAD_DIM, TILE), lambda bi, i: (0, i)),
        ],
        out_specs=[
            pl.BlockSpec((None, GQA_Q_W, TILE), lane_map),
            pl.BlockSpec((None, None, TILE, GQA_KV_W), chunk_map),
            pl.BlockSpec((None, None, GQA_KV_W, TILE), chunk_map),
            pl.BlockSpec((None, DIFF_QK_W, TILE), lane_map),
            pl.BlockSpec((None, None, TILE, DIFF_QK_W), chunk_map),
            pl.BlockSpec((None, None, DIFF_V_W, TILE), chunk_map),
            pl.BlockSpec((None, TILE, CONV_CH), tile_map),
        ],
        out_shape=[
            jax.ShapeDtypeStruct((b, GQA_Q_W, t), BF16),
            jax.ShapeDtypeStruct((b, nt, TILE, GQA_KV_W), BF16),
            jax.ShapeDtypeStruct((b, nt, GQA_KV_W, TILE), BF16),
            jax.ShapeDtypeStruct((b, DIFF_QK_W, t), BF16),
            jax.ShapeDtypeStruct((b, nt, TILE, DIFF_QK_W), BF16),
            jax.ShapeDtypeStruct((b, nt, DIFF_V_W, TILE), BF16),
            jax.ShapeDtypeStruct((b, t, CONV_CH), F32),
        ],
        compiler_params=_cparams(2),
        name="project",
    )(xs, modsel, g1, wt, wcu, gains, cos_t, sin_t)


def _attn_kernel(*refs, n_maps, diff, q_off, lam_init):
    if diff:
        qt_ref, k_ref, vt_ref, lam_ref, gsub_ref, o_ref, qpad, m_s, l_s, acc_s = refs
    else:
        qt_ref, k_ref, vt_ref, o_ref, qpad, m_s, l_s, acc_s = refs
    group = pl.program_id(1)
    q_tile = pl.program_id(2) + q_off

    zeros = jnp.zeros((HEAD_DIM, TILE), BF16)
    for j in range(n_maps):
        qj = qt_ref[HEAD_DIM * j:HEAD_DIM * (j + 1), :]
        if diff:
            halves = [qj, zeros] if j == 0 else [zeros, qj]
        else:
            qf = qj.astype(F32)
            halves = [(qf * (group == h).astype(F32)).astype(BF16) for h in range(GQA_KV_HEADS)]
        qpad[j] = jnp.concatenate(halves, axis=0)
    m_s[...] = jnp.full(m_s.shape, NEG_BIG, F32)
    l_s[...] = jnp.zeros(l_s.shape, F32)
    acc_s[...] = jnp.zeros(acc_s.shape, F32)

    n_chunks = jnp.where(q_tile == 0, 1, k_ref.shape[0])

    def body(c, carry):
        kc = k_ref[c]
        vc = vt_ref[c]
        for j in range(n_maps):
            s = _dot(kc, qpad[j])
            m_prev = m_s[j]
            m_new = jnp.maximum(m_prev, jnp.max(s, axis=0, keepdims=True))
            alpha = jnp.exp(m_prev - m_new)
            p = jnp.exp(s - m_new)
            l_s[j] = alpha * l_s[j] + jnp.sum(p, axis=0, keepdims=True)
            acc_s[j] = alpha * acc_s[j] + _dot(vc, p.astype(BF16))
            m_s[j] = m_new
        return carry

    lax.fori_loop(0, n_chunks, body, 0)

    if diff:
        lam_p = lam_ref[...]
        lam = (jnp.exp(jnp.sum(lam_p[0:1] * lam_p[1:2], axis=1, keepdims=True))
               - jnp.exp(jnp.sum(lam_p[2:3] * lam_p[3:4], axis=1, keepdims=True)) + lam_init)
        o = acc_s[0] / l_s[0] - lam * (acc_s[1] / l_s[1])
        ms = jnp.mean(o * o, axis=0, keepdims=True)
        o = o * lax.rsqrt(ms + EPS) * gsub_ref[...] * (1.0 - lam_init)
    else:
        o = jnp.concatenate([acc_s[j] / l_s[j] for j in range(n_maps)], axis=0)
    o_ref[...] = o.T.astype(BF16)


def _attention(qt, k4, vt4, *, diff, skip_ctx, lam_rows=None, gsub=None, lam_init=0.0):
    b, qw, t = qt.shape
    nt = t // TILE
    if diff:
        n_groups, n_maps, dv = DIFF_HEADS, 2, DIFF_V_DIM
        k_map = lambda bi, g, i: (bi, 0, 0, g)
    else:
        n_groups, n_maps, dv = GQA_KV_HEADS, GQA_GROUP, HEAD_DIM
        k_map = lambda bi, g, i: (bi, 0, 0, 0)
    q_off = 1 if skip_ctx else 0
    out_w = qw // n_groups
    in_specs = [
        pl.BlockSpec((None, n_maps * HEAD_DIM, TILE), lambda bi, g, i: (bi, g, i + q_off)),
        pl.BlockSpec((None, nt, TILE, 2 * HEAD_DIM), k_map),
        pl.BlockSpec((None, nt, dv, TILE), lambda bi, g, i: (bi, 0, g, 0)),
    ]
    args = [qt, k4, vt4]
    if diff:
        in_specs += [_const_spec((8, 128)), _const_spec((DIFF_V_DIM, TILE))]
        args += [lam_rows, gsub]
    return pl.pallas_call(
        functools.partial(_attn_kernel, n_maps=n_maps, diff=diff, q_off=q_off, lam_init=lam_init),
        grid=(b, n_groups, nt - q_off),
        in_specs=in_specs,
        out_specs=pl.BlockSpec((None, TILE, out_w), lambda bi, g, i: (bi, i + q_off, g)),
        out_shape=jax.ShapeDtypeStruct((b, t, qw), BF16),
        scratch_shapes=[
            pltpu.VMEM((n_maps, 2 * HEAD_DIM, TILE), BF16),
            pltpu.VMEM((n_maps, 1, TILE), F32),
            pltpu.VMEM((n_maps, 1, TILE), F32),
            pltpu.VMEM((n_maps, dv, TILE), F32),
        ],
        compiler_params=_cparams(3),
        name="diff_attention" if diff else "gqa_attention",
    )(*args)


def _merge_kernel(x_ref, mod_ref, g1_ref, oa_ref, ob_ref, u_ref, up_ref, un_ref,
                  wgate_ref, bgate_ref, wa_ref, wb_ref, wc_ref, wdw_ref, bdw_ref, gln_ref, bln_ref,
                  wout_ref, o_ref, ubuf, conv_s, *, tile_off, last_tile):
    x = x_ref[...]
    hb = _norm_mod(x, g1_ref[...], mod_ref[0:1, :], mod_ref[1:2, :]).astype(BF16)
    gates = jax.nn.sigmoid(_dot(hb, wgate_ref[...]) + bgate_ref[...])
    y_a = _dot(oa_ref[...], wa_ref[...])
    y_b = _dot(ob_ref[...], wb_ref[...])

    tile = pl.program_id(1) + tile_off
    has_prev = jnp.logical_and(tile != 0, tile != 1).astype(F32)
    has_next = jnp.logical_and(tile != 0, tile != last_tile).astype(F32)
    ubuf[0:HALO, :] = up_ref[...] * has_prev
    ubuf[HALO:HALO + TILE, :] = u_ref[...]
    ubuf[HALO + TILE:, :] = un_ref[...] * has_next
    rows = 32
    for r in range(0, TILE, rows):
        acc = jnp.zeros((rows, CONV_CH), F32) + bdw_ref[...]
        for j in range(CONV_WIDTH):
            start = r + HALO - CONV_HALF + j
            acc = acc + ubuf[start:start + rows, :] * wdw_ref[j:j + 1, :]
        conv_s[r:r + rows, :] = acc
    v = conv_s[...]
    mu = jnp.mean(v, axis=-1, keepdims=True)
    vc = v - mu
    var = jnp.mean(vc * vc, axis=-1, keepdims=True)
    v = vc * lax.rsqrt(var + EPS) * gln_ref[...] + bln_ref[...]
    v = v * jax.nn.sigmoid(v)
    y_c = _dot(v.astype(BF16), wc_ref[...])

    merged = (gates[:, 0:D_MODEL] * y_a + gates[:, D_MODEL:2 * D_MODEL] * y_b
              + gates[:, 2 * D_MODEL:] * y_c)
    y = _dot(merged.astype(BF16), wout_ref[...])
    o_ref[...] = x + mod_ref[2:3, :] * y


def _merge(xs, modsel, g1, oa, ob, u, wgate, bgate, wa, wb, wc, wdw, bdw, gln, bln, wout, *, skip_ctx):
    b, t, d = xs.shape
    nt = t // TILE
    off = 1 if skip_ctx else 0
    hb_per_tile = TILE // HALO
    n_halo_blocks = t // HALO
    tile_map = lambda bi, i: (bi, i + off, 0)
    return pl.pallas_call(
        functools.partial(_merge_kernel, tile_off=off, last_tile=nt - 1),
        grid=(b, nt - off),
        in_specs=[
            pl.BlockSpec((None, TILE, d), tile_map),
            pl.BlockSpec((None, None, 6, d), lambda bi, i: (bi, jnp.minimum(i + off, 1), 0, 0)),
            _const_spec((1, d)),
            pl.BlockSpec((None, TILE, GQA_Q_W), tile_map),
            pl.BlockSpec((None, TILE, DIFF_V_W), tile_map),
            pl.BlockSpec((None, TILE, CONV_CH), tile_map),
            pl.BlockSpec((None, HALO, CONV_CH),
                         lambda bi, i: (bi, jnp.maximum((i + off) * hb_per_tile - 1, 0), 0)),
            pl.BlockSpec((None, HALO, CONV_CH),
                         lambda bi, i: (bi, jnp.minimum((i + off + 1) * hb_per_tile, n_halo_blocks - 1), 0)),
            _const_spec((d, N_BRANCH * d)),
            _const_spec((1, N_BRANCH * d)),
            _const_spec((GQA_Q_W, d)),
            _const_spec((DIFF_V_W, d)),
            _const_spec((CONV_CH, d)),
            _const_spec((CONV_WIDTH + 1, CONV_CH)),
            _const_spec((1, CONV_CH)),
            _const_spec((1, CONV_CH)),
            _const_spec((1, CONV_CH)),
            _const_spec((d, d)),
        ],
        out_specs=pl.BlockSpec((None, TILE, d), tile_map),
        out_shape=jax.ShapeDtypeStruct(xs.shape, F32),
        scratch_shapes=[
            pltpu.VMEM((TILE + 2 * HALO, CONV_CH), F32),
            pltpu.VMEM((TILE, CONV_CH), F32),
        ],
        input_output_aliases={0: 0},
        compiler_params=_cparams(2),
        name="merge",
    )(xs, modsel, g1, oa, ob, u, u, u, wgate, bgate, wa, wb, wc, wdw, bdw, gln, bln, wout)


def _ffn_kernel(x_ref, mod_ref, g2_ref, win_ref, wout_ref, o_ref):
    x = x_ref[...]
    hb = _norm_mod(x, g2_ref[...], mod_ref[3:4, :], mod_ref[4:5, :]).astype(BF16)
    gate = _dot(hb, win_ref[:, :D_FF])
    up = _dot(hb, win_ref[:, D_FF:])
    act = (gate * jax.nn.sigmoid(gate) * up).astype(BF16)
    o_ref[...] = x + mod_ref[5:6, :] * _dot(act, wout_ref[...])


def _ffn(xs, modsel, g2, win, wout, *, skip_ctx):
    b, t, d = xs.shape
    nt = t // TILE
    off = 1 if skip_ctx else 0
    kwargs = {}
    if skip_ctx:
        out_shape = jax.ShapeDtypeStruct((b, t - TILE, d), F32)
    else:
        out_shape = jax.ShapeDtypeStruct(xs.shape, F32)
        kwargs["input_output_aliases"] = {0: 0}
    return pl.pallas_call(
        _ffn_kernel,
        grid=(b, nt - off),
        in_specs=[
            pl.BlockSpec((None, TILE, d), lambda bi, i: (bi, i + off, 0)),
            pl.BlockSpec((None, None, 6, d), lambda bi, i: (bi, jnp.minimum(i + off, 1), 0, 0)),
            _const_spec((1, d)),
            _const_spec((d, 2 * D_FF)),
            _const_spec((D_FF, d)),
        ],
        out_specs=pl.BlockSpec((None, TILE, d), lambda bi, i: (bi, i, 0)),
        out_shape=out_shape,
        compiler_params=_cparams(2),
        name="swiglu",
        **kwargs,
    )(xs, modsel, g2, win, wout)


def _rope_tables_t(n_ctx, n_lat):
    n_freq = HEAD_DIM // 4
    pos = jnp.arange(n_lat, dtype=jnp.int32)
    row = (pos // GRID_W).astype(F32)
    col = (pos % GRID_W).astype(F32)
    inv = ROPE_THETA ** (-jnp.arange(n_freq, dtype=F32) / n_freq)
    ang_r = inv[:, None] * row[None, :]
    ang_c = inv[:, None] * col[None, :]
    cos = jnp.concatenate([jnp.cos(ang_r)] * 2 + [jnp.cos(ang_c)] * 2, axis=0)
    sin = jnp.concatenate([-jnp.sin(ang_r), jnp.sin(ang_r), -jnp.sin(ang_c), jnp.sin(ang_c)], axis=0)
    cos = jnp.concatenate([jnp.ones((HEAD_DIM, n_ctx), F32), cos], axis=1)
    sin = jnp.concatenate([jnp.zeros((HEAD_DIM, n_ctx), F32), sin], axis=1)
    return cos, sin


def kernel(x, c, ctx, c_ctx, w_mod, b_mod, g_norm1, w_in, w_gate, b_gate, g_q_gqa, g_k_gqa, g_q_diff, g_k_diff, lambda_q1, lambda_k1, lambda_q2, lambda_k2, g_subln, w_dw, b_dw, g_conv_ln, b_conv_ln, w_a, w_b, w_c, w_out, g_norm2, w_ffn_in, w_ffn_out):
    batch, n_lat, d = x.shape
    n_ctx = ctx.shape[1]
    depth = w_mod.shape[0]
    assert n_ctx == TILE and n_lat % TILE == 0 and d == D_MODEL and batch + 1 <= MOD_ROWS

    cvec = jnp.concatenate(
        [c, c_ctx[None, :], jnp.zeros((MOD_ROWS - batch - 1, d), F32)], axis=0)
    mods = _modulation(cvec, w_mod, b_mod)
    cos_t, sin_t = _rope_tables_t(n_ctx, n_lat)
    xs = jnp.concatenate([ctx, x], axis=1)

    for l in range(depth):
        last = l == depth - 1
        lam_init = 0.8 - 0.6 * math.exp(-0.3 * l)
        lat_mod = mods[l, :batch].reshape(batch, 1, 6, d)
        ctx_mod = jnp.broadcast_to(mods[l, batch].reshape(1, 1, 6, d), (batch, 1, 6, d))
        modsel = jnp.concatenate([ctx_mod, lat_mod], axis=1)
        gains = jnp.broadcast_to(
            jnp.stack([g_q_gqa[l], g_k_gqa[l], g_q_diff[l], g_k_diff[l]])[:, :, None],
            (4, HEAD_DIM, TILE))
        lam_rows = jnp.zeros((8, 128), F32).at[0:4, 0:HEAD_DIM].set(
            jnp.stack([lambda_q1[l], lambda_k1[l], lambda_q2[l], lambda_k2[l]]))
        gsub = jnp.broadcast_to(g_subln[l][:, None], (DIFF_V_DIM, TILE))
        g1 = g_norm1[l].reshape(1, d)
        wt = w_in[l, :, :HEADS_W].T.astype(BF16)
        wcu = w_in[l, :, HEADS_W:].astype(BF16)
        wdw = jnp.concatenate([w_dw[l], jnp.zeros((1, CONV_CH), F32)], axis=0)

        qt, k4, vt4, dqt, dk4, dvt4, u = _project(xs, modsel, g1, wt, wcu, gains, cos_t, sin_t)
        o_gqa = _attention(qt, k4, vt4, diff=False, skip_ctx=last)
        o_diff = _attention(dqt, dk4, dvt4, diff=True, skip_ctx=last,
                            lam_rows=lam_rows, gsub=gsub, lam_init=lam_init)
        xs = _merge(xs, modsel, g1, o_gqa, o_diff, u,
                    w_gate[l].astype(BF16), b_gate[l].reshape(1, -1),
                    w_a[l].astype(BF16), w_b[l].astype(BF16), w_c[l].astype(BF16),
                    wdw, b_dw[l].reshape(1, -1), g_conv_ln[l].reshape(1, -1),
                    b_conv_ln[l].reshape(1, -1), w_out[l].astype(BF16), skip_ctx=last)
        xs = _ffn(xs, modsel, g_norm2[l].reshape(1, d), w_ffn_in[l].astype(BF16),
                  w_ffn_out[l].astype(BF16), skip_ctx=last)
    return xs
```

```python
import functools
import math

import jax
import jax.numpy as jnp
from jax import lax
from jax.experimental import pallas as pl
from jax.experimental.pallas import tpu as pltpu

F32 = jnp.float32
BF16 = jnp.bfloat16

D_MODEL = 1024
HEAD_DIM = 64
GQA_Q_HEADS = 8
GQA_KV_HEADS = 2
GQA_GROUP = GQA_Q_HEADS // GQA_KV_HEADS
DIFF_HEADS = 4
DIFF_V_DIM = 2 * HEAD_DIM
CONV_CH = 512
CONV_WIDTH = 31
CONV_HALF = CONV_WIDTH // 2
D_FF = 2816
GRID_W = 64
ROPE_THETA = 10000.0
EPS = 1e-6
ATTN_SCALE = HEAD_DIM ** -0.5
N_BRANCH = 3

GQA_Q_W = GQA_Q_HEADS * HEAD_DIM
GQA_KV_W = GQA_KV_HEADS * HEAD_DIM
DIFF_QK_W = DIFF_HEADS * 2 * HEAD_DIM
DIFF_V_W = DIFF_HEADS * DIFF_V_DIM
CONV_IN_W = 2 * CONV_CH
OFF_GQ = 0
OFF_GK = OFF_GQ + GQA_Q_W
OFF_GV = OFF_GK + GQA_KV_W
OFF_DQ = OFF_GV + GQA_KV_W
OFF_DK = OFF_DQ + DIFF_QK_W
OFF_DV = OFF_DK + DIFF_QK_W
OFF_CU = OFF_DV + DIFF_V_W
D_IN = OFF_CU + CONV_IN_W
HEADS_W = OFF_CU

TILE = 256
HALO = 16
MOD_ROWS = 16
MOD_BLOCK_N = 1536
NEG_BIG = -1e30
LOG2E = math.log2(math.e)
Q_SCALE = ATTN_SCALE * LOG2E
SCORE_BOUND_FAST = 80.0
KEY_BLOCK_CHUNKS = 2
VMEM_LIMIT = 52 * 1024 * 1024


def _cparams(n_axes):
    return pltpu.CompilerParams(
        dimension_semantics=("arbitrary",) * n_axes, vmem_limit_bytes=VMEM_LIMIT)


def _const_spec(shape):
    nd = len(shape)
    return pl.BlockSpec(shape, lambda *_: (0,) * nd, pipeline_mode=pl.Buffered(1))


def _dot(a, b):
    return jnp.dot(a, b, preferred_element_type=F32)


def _split_bf16(a):
    hi = a.astype(BF16)
    lo = (a - hi.astype(F32)).astype(BF16)
    return hi, lo


def _norm_mod(x, g, shift, scale):
    ms = jnp.mean(x * x, axis=-1, keepdims=True)
    y = x * lax.rsqrt(ms + EPS) * g
    return y * (1.0 + scale) + shift


def _mod_kernel(c_ref, w_ref, b_ref, o_ref):
    c = c_ref[...]
    a = c * jax.nn.sigmoid(c)
    a_hi, a_lo = _split_bf16(a)
    w_hi, w_lo = _split_bf16(w_ref[...])
    o_ref[...] = _dot(a_hi, w_hi) + _dot(a_hi, w_lo) + _dot(a_lo, w_hi) + b_ref[...]


def _modulation(cvec, w_mod, b_mod):
    depth, d, n = w_mod.shape
    return pl.pallas_call(
        _mod_kernel,
        grid=(depth, n // MOD_BLOCK_N),
        in_specs=[
            pl.BlockSpec((MOD_ROWS, d), lambda l, j: (0, 0)),
            pl.BlockSpec((None, d, MOD_BLOCK_N), lambda l, j: (l, 0, j)),
            pl.BlockSpec((None, 1, MOD_BLOCK_N), lambda l, j: (l, 0, j)),
        ],
        out_specs=pl.BlockSpec((None, MOD_ROWS, MOD_BLOCK_N), lambda l, j: (l, 0, j)),
        out_shape=jax.ShapeDtypeStruct((depth, MOD_ROWS, n), F32),
        compiler_params=_cparams(2),
        name="modulation",
    )(cvec, w_mod, b_mod.reshape(depth, 1, n))


def _proj_kernel(x_ref, mod_ref, g1_ref, wt_ref, wcu_ref, gains_ref, cos_ref, sin_ref,
                 qt_ref, k_ref, vt_ref, dqt_ref, dk_ref, dvt_ref, u_ref):
    hb = _norm_mod(x_ref[...], g1_ref[...], mod_ref[0:1, :], mod_ref[1:2, :]).astype(BF16)
    pt = lax.dot_general(wt_ref[...], hb, (((1,), (1,)), ((), ())), preferred_element_type=F32)
    cos = cos_ref[...][None]
    sin = sin_ref[...][None]

    def norm_rope(t, gain_idx, scale):
        n = t.shape[0] // HEAD_DIM
        t3 = t.reshape(n, HEAD_DIM, TILE)
        ms = jnp.mean(t3 * t3, axis=1, keepdims=True)
        y = t3 * lax.rsqrt(ms + EPS) * gains_ref[gain_idx][None]
        q = HEAD_DIM // 4
        swapped = jnp.concatenate(
            [y[:, q:2 * q], y[:, 0:q], y[:, 3 * q:4 * q], y[:, 2 * q:3 * q]], axis=1)
        y = y * cos + swapped * sin
        if scale != 1.0:
            y = y * scale
        return y.reshape(t.shape)

    qt_ref[...] = norm_rope(pt[OFF_GQ:OFF_GK], 0, Q_SCALE).astype(BF16)
    k_ref[...] = norm_rope(pt[OFF_GK:OFF_GV], 1, 1.0).T.astype(BF16)
    vt_ref[...] = pt[OFF_GV:OFF_DQ].astype(BF16)
    dqt_ref[...] = norm_rope(pt[OFF_DQ:OFF_DK], 2, Q_SCALE).astype(BF16)
    dk_ref[...] = norm_rope(pt[OFF_DK:OFF_DV], 3, 1.0).T.astype(BF16)
    dvt_ref[...] = pt[OFF_DV:OFF_CU].astype(BF16)
    cu = _dot(hb, wcu_ref[...])
    u_ref[...] = cu[:, :CONV_CH] * jax.nn.sigmoid(cu[:, CONV_CH:])


def _project(xs, modsel, g1, wt, wcu, gains, cos_t, sin_t):
    b, t, d = xs.shape
    nt = t // TILE
    tile_map = lambda bi, i: (bi, i, 0)
    chunk_map = lambda bi, i: (bi, i, 0, 0)
    lane_map = lambda bi, i: (bi, 0, i)
    return pl.pallas_call(
        _proj_kernel,
        grid=(b, nt),
        in_specs=[
            pl.BlockSpec((None, TILE, d), tile_map),
            pl.BlockSpec((None, None, 6, d), lambda bi, i: (bi, jnp.minimum(i, 1), 0, 0)),
            _const_spec((1, d)),
            _const_spec((HEADS_W, d)),
            _const_spec((d, CONV_IN_W)),
            _const_spec((4, HEAD_DIM, TILE)),
            pl.BlockSpec((HEAD_DIM, TILE), lambda bi, i: (0, i)),
            pl.BlockSpec((HEAD_DIM, TILE), lambda bi, i: (0, i)),
        ],
        out_specs=[
            pl.BlockSpec((None, GQA_Q_W, TILE), lane_map),
            pl.BlockSpec((None, None, TILE, GQA_KV_W), chunk_map),
            pl.BlockSpec((None, None, GQA_KV_W, TILE), chunk_map),
            pl.BlockSpec((None, DIFF_QK_W, TILE), lane_map),
            pl.BlockSpec((None, None, TILE, DIFF_QK_W), chunk_map),
            pl.BlockSpec((None, None, DIFF_V_W, TILE), chunk_map),
            pl.BlockSpec((None, TILE, CONV_CH), tile_map),
        ],
        out_shape=[
            jax.ShapeDtypeStruct((b, GQA_Q_W, t), BF16),
            jax.ShapeDtypeStruct((b, nt, TILE, GQA_KV_W), BF16),
            jax.ShapeDtypeStruct((b, nt, GQA_KV_W, TILE), BF16),
            jax.ShapeDtypeStruct((b, DIFF_QK_W, t), BF16),
            jax.ShapeDtypeStruct((b, nt, TILE, DIFF_QK_W), BF16),
            jax.ShapeDtypeStruct((b, nt, DIFF_V_W, TILE), BF16),
            jax.ShapeDtypeStruct((b, t, CONV_CH), F32),
        ],
        compiler_params=_cparams(2),
        name="project",
    )(xs, modsel, g1, wt, wcu, gains, cos_t, sin_t)


def _stage_padded_q(qt_ref, qpad, n_maps, diff):
    group = pl.program_id(1)
    zeros = jnp.zeros((HEAD_DIM, TILE), BF16)
    for j in range(n_maps):
        qj = qt_ref[HEAD_DIM * j:HEAD_DIM * (j + 1), :]
        if diff:
            halves = [qj, zeros] if j == 0 else [zeros, qj]
        else:
            qf = qj.astype(F32)
            halves = [(qf * (group == h).astype(F32)).astype(BF16) for h in range(GQA_KV_HEADS)]
        qpad[j] = jnp.concatenate(halves, axis=0)


def _finish_attention(o_ref, outs, diff, lam_ref, gsub_ref, lam_init):
    if diff:
        lam_p = lam_ref[...]
        lam = (jnp.exp(jnp.sum(lam_p[0:1] * lam_p[1:2], axis=1, keepdims=True))
               - jnp.exp(jnp.sum(lam_p[2:3] * lam_p[3:4], axis=1, keepdims=True)) + lam_init)
        o = outs[0] - lam * outs[1]
        ms = jnp.mean(o * o, axis=0, keepdims=True)
        o = o * lax.rsqrt(ms + EPS) * gsub_ref[...] * (1.0 - lam_init)
    else:
        o = jnp.concatenate(outs, axis=0)
    o_ref[...] = o.T.astype(BF16)


def _attn_safe_kernel(*refs, n_maps, diff, q_off, lam_init):
    if diff:
        qt_ref, k_ref, vt_ref, lam_ref, gsub_ref, o_ref, qpad, m_s, l_s, acc_s = refs
    else:
        qt_ref, k_ref, vt_ref, o_ref, qpad, m_s, l_s, acc_s = refs
        lam_ref = gsub_ref = None
    q_tile = pl.program_id(2) + q_off
    _stage_padded_q(qt_ref, qpad, n_maps, diff)
    m_s[...] = jnp.full(m_s.shape, NEG_BIG, F32)
    l_s[...] = jnp.zeros(l_s.shape, F32)
    acc_s[...] = jnp.zeros(acc_s.shape, F32)

    n_chunks = jnp.where(q_tile == 0, 1, k_ref.shape[0])

    def body(c, carry):
        kc = k_ref[c]
        vc = vt_ref[c]
        for j in range(n_maps):
            s = _dot(kc, qpad[j])
            m_prev = m_s[j]
            m_new = jnp.maximum(m_prev, jnp.max(s, axis=0, keepdims=True))
            alpha = jnp.exp2(m_prev - m_new)
            p = jnp.exp2(s - m_new)
            l_s[j] = alpha * l_s[j] + jnp.sum(p, axis=0, keepdims=True)
            acc_s[j] = alpha * acc_s[j] + _dot(vc, p.astype(BF16))
            m_s[j] = m_new
        return carry

    lax.fori_loop(0, n_chunks, body, 0)
    _finish_attention(o_ref, [acc_s[j] / l_s[j] for j in range(n_maps)],
                      diff, lam_ref, gsub_ref, lam_init)


def _attn_fast_kernel(*refs, n_maps, dv, diff, q_off, lam_init, key_block):
    if diff:
        qt_ref, k_ref, vt_ref, lam_ref, gsub_ref, o_ref, qpad, s_a, s_b, acc_s = refs
    else:
        qt_ref, k_ref, vt_ref, o_ref, qpad, s_a, s_b, acc_s = refs
        lam_ref = gsub_ref = None
    q_tile = pl.program_id(2) + q_off
    chunks_per_block = key_block // TILE
    n_blocks = (k_ref.shape[0] - 1) // chunks_per_block
    _stage_padded_q(qt_ref, qpad, n_maps, diff)

    def v_ext(first_chunk, n):
        v = jnp.concatenate([vt_ref[first_chunk + i] for i in range(n)], axis=1)
        row = lax.broadcasted_iota(jnp.int32, (16, v.shape[1]), 0)
        ones_row = jnp.where(row == 0, 1.0, 0.0).astype(BF16)
        return jnp.concatenate([v, ones_row], axis=0)

    def k_block(i):
        kb = k_ref[pl.ds(1 + i * chunks_per_block, chunks_per_block)]
        return kb.reshape(key_block, 2 * HEAD_DIM)

    def probs(s):
        return jnp.exp2(s).astype(BF16)

    kc = k_ref[0]
    ve = v_ext(0, 1)
    for j in range(n_maps):
        acc_s[j] = _dot(ve, probs(_dot(kc, qpad[j])))

    def step(src, dst, i, has_next):
        if has_next:
            k_next = k_block(i + 1)
        ve_cur = v_ext(1 + i * chunks_per_block, chunks_per_block)
        for j in range(n_maps):
            if has_next:
                dst[j] = _dot(k_next, qpad[j])
            acc_s[j] += _dot(ve_cur, probs(src[j]))

    def latent_keys():
        k0 = k_block(0)
        for j in range(n_maps):
            s_a[j] = _dot(k0, qpad[j])

        def body(t, carry):
            step(s_a, s_b, 2 * t, True)
            step(s_b, s_a, 2 * t + 1, True)
            return carry

        lax.fori_loop(0, n_blocks // 2 - 1, body, 0)
        step(s_a, s_b, n_blocks - 2, True)
        step(s_b, None, n_blocks - 1, False)

    if q_off == 0:
        pl.when(q_tile != 0)(latent_keys)
    else:
        latent_keys()

    outs = []
    for j in range(n_maps):
        acc = acc_s[j]
        outs.append(acc[0:dv] / acc[dv:dv + 1])
    _finish_attention(o_ref, outs, diff, lam_ref, gsub_ref, lam_init)


def _attention_call(qt, k4, vt4, extra, *, diff, skip_ctx, lam_init, fast):
    b, qw, t = qt.shape
    nt = t // TILE
    if diff:
        n_groups, n_maps, dv = DIFF_HEADS, 2, DIFF_V_DIM
        k_map = lambda bi, g, i: (bi, 0, 0, g)
    else:
        n_groups, n_maps, dv = GQA_KV_HEADS, GQA_GROUP, HEAD_DIM
        k_map = lambda bi, g, i: (bi, 0, 0, 0)
    q_off = 1 if skip_ctx else 0
    out_w = qw // n_groups
    in_specs = [
        pl.BlockSpec((None, n_maps * HEAD_DIM, TILE), lambda bi, g, i: (bi, g, i + q_off)),
        pl.BlockSpec((None, nt, TILE, 2 * HEAD_DIM), k_map),
        pl.BlockSpec((None, nt, dv, TILE), lambda bi, g, i: (bi, 0, g, 0)),
    ]
    if diff:
        in_specs += [_const_spec((8, 128)), _const_spec((DIFF_V_DIM, TILE))]
    qpad_scratch = pltpu.VMEM((n_maps, 2 * HEAD_DIM, TILE), BF16)
    if fast:
        key_block = _key_block(nt - 1)
        body = functools.partial(_attn_fast_kernel, n_maps=n_maps, dv=dv, diff=diff, q_off=q_off,
                                 lam_init=lam_init, key_block=key_block)
        scratch = [qpad_scratch,
                   pltpu.VMEM((n_maps, key_block, TILE), F32),
                   pltpu.VMEM((n_maps, key_block, TILE), F32),
                   pltpu.VMEM((n_maps, dv + 16, TILE), F32)]
    else:
        body = functools.partial(_attn_safe_kernel, n_maps=n_maps, diff=diff, q_off=q_off,
                                 lam_init=lam_init)
        scratch = [qpad_scratch,
                   pltpu.VMEM((n_maps, 1, TILE), F32),
                   pltpu.VMEM((n_maps, 1, TILE), F32),
                   pltpu.VMEM((n_maps, dv, TILE), F32)]
    kind = ("diff" if diff else "gqa") + ("_attention" if fast else "_attention_safe")
    return pl.pallas_call(
        body,
        grid=(b, n_groups, nt - q_off),
        in_specs=in_specs,
        out_specs=pl.BlockSpec((None, TILE, out_w), lambda bi, g, i: (bi, i + q_off, g)),
        out_shape=jax.ShapeDtypeStruct((b, t, qw), BF16),
        scratch_shapes=scratch,
        compiler_params=_cparams(3),
        name=kind,
    )(qt, k4, vt4, *extra)


def _key_block(n_latent_chunks):
    assert n_latent_chunks % 2 == 0
    chunks = KEY_BLOCK_CHUNKS if n_latent_chunks % (2 * KEY_BLOCK_CHUNKS) == 0 else 1
    return chunks * TILE


def _attention(qt, k4, vt4, g_q, g_k, *, diff, skip_ctx, extra=(), lam_init=0.0):
    bound = (HEAD_DIM * ATTN_SCALE * LOG2E * (1.0 + 2.0 ** -8) ** 2
             * jnp.max(jnp.abs(g_q)) * jnp.max(jnp.abs(g_k)))
    call = functools.partial(_attention_call, diff=diff, skip_ctx=skip_ctx, lam_init=lam_init)
    return lax.cond(bound <= SCORE_BOUND_FAST,
                    lambda *a: call(*a[:3], a[3:], fast=True),
                    lambda *a: call(*a[:3], a[3:], fast=False),
                    qt, k4, vt4, *extra)


def _merge_kernel(x_ref, mod_ref, g1_ref, oa_ref, ob_ref, u_ref, up_ref, un_ref,
                  wgate_ref, bgate_ref, wa_ref, wb_ref, wc_ref, wdw_ref, bdw_ref, gln_ref, bln_ref,
                  wout_ref, o_ref, ubuf, conv_s, *, tile_off, last_tile):
    x = x_ref[...]
    hb = _norm_mod(x, g1_ref[...], mod_ref[0:1, :], mod_ref[1:2, :]).astype(BF16)
    gates = jax.nn.sigmoid(_dot(hb, wgate_ref[...]) + bgate_ref[...])
    y_a = _dot(oa_ref[...], wa_ref[...])
    y_b = _dot(ob_ref[...], wb_ref[...])

    tile = pl.program_id(1) + tile_off
    has_prev = jnp.logical_and(tile != 0, tile != 1).astype(F32)
    has_next = jnp.logical_and(tile != 0, tile != last_tile).astype(F32)
    ubuf[0:HALO, :] = up_ref[...] * has_prev
    ubuf[HALO:HALO + TILE, :] = u_ref[...]
    ubuf[HALO + TILE:, :] = un_ref[...] * has_next
    rows = 32
    for r in range(0, TILE, rows):
        acc = jnp.zeros((rows, CONV_CH), F32) + bdw_ref[...]
        for j in range(CONV_WIDTH):
            start = r + HALO - CONV_HALF + j
            acc = acc + ubuf[start:start + rows, :] * wdw_ref[j:j + 1, :]
        conv_s[r:r + rows, :] = acc
    v = conv_s[...]
    mu = jnp.mean(v, axis=-1, keepdims=True)
    vc = v - mu
    var = jnp.mean(vc * vc, axis=-1, keepdims=True)
    v = vc * lax.rsqrt(var + EPS) * gln_ref[...] + bln_ref[...]
    v = v * jax.nn.sigmoid(v)
    y_c = _dot(v.astype(BF16), wc_ref[...])

    merged = (gates[:, 0:D_MODEL] * y_a + gates[:, D_MODEL:2 * D_MODEL] * y_b
              + gates[:, 2 * D_MODEL:] * y_c)
    y = _dot(merged.astype(BF16), wout_ref[...])
    o_ref[...] = x + mod_ref[2:3, :] * y


def _merge(xs, modsel, g1, oa, ob, u, wgate, bgate, wa, wb, wc, wdw, bdw, gln, bln, wout, *, skip_ctx):
    b, t, d = xs.shape
    nt = t // TILE
    off = 1 if skip_ctx else 0
    hb_per_tile = TILE // HALO
    n_halo_blocks = t // HALO
    tile_map = lambda bi, i: (bi, i + off, 0)
    return pl.pallas_call(
        functools.partial(_merge_kernel, tile_off=off, last_tile=nt - 1),
        grid=(b, nt - off),
        in_specs=[
            pl.BlockSpec((None, TILE, d), tile_map),
            pl.BlockSpec((None, None, 6, d), lambda bi, i: (bi, jnp.minimum(i + off, 1), 0, 0)),
            _const_spec((1, d)),
            pl.BlockSpec((None, TILE, GQA_Q_W), tile_map),
            pl.BlockSpec((None, TILE, DIFF_V_W), tile_map),
            pl.BlockSpec((None, TILE, CONV_CH), tile_map),
            pl.BlockSpec((None, HALO, CONV_CH),
                         lambda bi, i: (bi, jnp.maximum((i + off) * hb_per_tile - 1, 0), 0)),
            pl.BlockSpec((None, HALO, CONV_CH),
                         lambda bi, i: (bi, jnp.minimum((i + off + 1) * hb_per_tile, n_halo_blocks - 1), 0)),
            _const_spec((d, N_BRANCH * d)),
            _const_spec((1, N_BRANCH * d)),
            _const_spec((GQA_Q_W, d)),
            _const_spec((DIFF_V_W, d)),
            _const_spec((CONV_CH, d)),
            _const_spec((CONV_WIDTH + 1, CONV_CH)),
            _const_spec((1, CONV_CH)),
            _const_spec((1, CONV_CH)),
            _const_spec((1, CONV_CH)),
            _const_spec((d, d)),
        ],
        out_specs=pl.BlockSpec((None, TILE, d), tile_map),
        out_shape=jax.ShapeDtypeStruct(xs.shape, F32),
        scratch_shapes=[
            pltpu.VMEM((TILE + 2 * HALO, CONV_CH), F32),
            pltpu.VMEM((TILE, CONV_CH), F32),
        ],
        input_output_aliases={0: 0},
        compiler_params=_cparams(2),
        name="merge",
    )(xs, modsel, g1, oa, ob, u, u, u, wgate, bgate, wa, wb, wc, wdw, bdw, gln, bln, wout)


def _ffn_kernel(x_ref, mod_ref, g2_ref, win_ref, wout_ref, o_ref):
    x = x_ref[...]
    hb = _norm_mod(x, g2_ref[...], mod_ref[3:4, :], mod_ref[4:5, :]).astype(BF16)
    gate = _dot(hb, win_ref[:, :D_FF])
    up = _dot(hb, win_ref[:, D_FF:])
    act = (gate * jax.nn.sigmoid(gate) * up).astype(BF16)
    o_ref[...] = x + mod_ref[5:6, :] * _dot(act, wout_ref[...])


def _ffn(xs, modsel, g2, win, wout, *, skip_ctx):
    b, t, d = xs.shape
    nt = t // TILE
    off = 1 if skip_ctx else 0
    kwargs = {}
    if skip_ctx:
        out_shape = jax.ShapeDtypeStruct((b, t - TILE, d), F32)
    else:
        out_shape = jax.ShapeDtypeStruct(xs.shape, F32)
        kwargs["input_output_aliases"] = {0: 0}
    return pl.pallas_call(
        _ffn_kernel,
        grid=(b, nt - off),
        in_specs=[
            pl.BlockSpec((None, TILE, d), lambda bi, i: (bi, i + off, 0)),
            pl.BlockSpec((None, None, 6, d), lambda bi, i: (bi, jnp.minimum(i + off, 1), 0, 0)),
            _const_spec((1, d)),
            _const_spec((d, 2 * D_FF)),
            _const_spec((D_FF, d)),
        ],
        out_specs=pl.BlockSpec((None, TILE, d), lambda bi, i: (bi, i, 0)),
        out_shape=out_shape,
        compiler_params=_cparams(2),
        name="swiglu",
        **kwargs,
    )(xs, modsel, g2, win, wout)


def _rope_tables_t(n_ctx, n_lat):
    n_freq = HEAD_DIM // 4
    pos = jnp.arange(n_lat, dtype=jnp.int32)
    row = (pos // GRID_W).astype(F32)
    col = (pos % GRID_W).astype(F32)
    inv = ROPE_THETA ** (-jnp.arange(n_freq, dtype=F32) / n_freq)
    ang_r = inv[:, None] * row[None, :]
    ang_c = inv[:, None] * col[None, :]
    cos = jnp.concatenate([jnp.cos(ang_r)] * 2 + [jnp.cos(ang_c)] * 2, axis=0)
    sin = jnp.concatenate([-jnp.sin(ang_r), jnp.sin(ang_r), -jnp.sin(ang_c), jnp.sin(ang_c)], axis=0)
    cos = jnp.concatenate([jnp.ones((HEAD_DIM, n_ctx), F32), cos], axis=1)
    sin = jnp.concatenate([jnp.zeros((HEAD_DIM, n_ctx), F32), sin], axis=1)
    return cos, sin


def kernel(x, c, ctx, c_ctx, w_mod, b_mod, g_norm1, w_in, w_gate, b_gate, g_q_gqa, g_k_gqa, g_q_diff, g_k_diff, lambda_q1, lambda_k1, lambda_q2, lambda_k2, g_subln, w_dw, b_dw, g_conv_ln, b_conv_ln, w_a, w_b, w_c, w_out, g_norm2, w_ffn_in, w_ffn_out):
    batch, n_lat, d = x.shape
    n_ctx = ctx.shape[1]
    depth = w_mod.shape[0]
    assert n_ctx == TILE and n_lat % TILE == 0 and d == D_MODEL and batch + 1 <= MOD_ROWS

    cvec = jnp.concatenate(
        [c, c_ctx[None, :], jnp.zeros((MOD_ROWS - batch - 1, d), F32)], axis=0)
    mods = _modulation(cvec, w_mod, b_mod)
    cos_t, sin_t = _rope_tables_t(n_ctx, n_lat)
    xs = jnp.concatenate([ctx, x], axis=1)

    for l in range(depth):
        last = l == depth - 1
        lam_init = 0.8 - 0.6 * math.exp(-0.3 * l)
        lat_mod = mods[l, :batch].reshape(batch, 1, 6, d)
        ctx_mod = jnp.broadcast_to(mods[l, batch].reshape(1, 1, 6, d), (batch, 1, 6, d))
        modsel = jnp.concatenate([ctx_mod, lat_mod], axis=1)
        gains = jnp.broadcast_to(
            jnp.stack([g_q_gqa[l], g_k_gqa[l], g_q_diff[l], g_k_diff[l]])[:, :, None],
            (4, HEAD_DIM, TILE))
        lam_rows = jnp.zeros((8, 128), F32).at[0:4, 0:HEAD_DIM].set(
            jnp.stack([lambda_q1[l], lambda_k1[l], lambda_q2[l], lambda_k2[l]]))
        gsub = jnp.broadcast_to(g_subln[l][:, None], (DIFF_V_DIM, TILE))
        g1 = g_norm1[l].reshape(1, d)
        wt = w_in[l, :, :HEADS_W].T.astype(BF16)
        wcu = w_in[l, :, HEADS_W:].astype(BF16)
        wdw = jnp.concatenate([w_dw[l], jnp.zeros((1, CONV_CH), F32)], axis=0)

        qt, k4, vt4, dqt, dk4, dvt4, u = _project(xs, modsel, g1, wt, wcu, gains, cos_t, sin_t)
        o_gqa = _attention(qt, k4, vt4, g_q_gqa[l], g_k_gqa[l], diff=False, skip_ctx=last)
        o_diff = _attention(dqt, dk4, dvt4, g_q_diff[l], g_k_diff[l], diff=True, skip_ctx=last,
                            extra=(lam_rows, gsub), lam_init=lam_init)
        xs = _merge(xs, modsel, g1, o_gqa, o_diff, u,
                    w_gate[l].astype(BF16), b_gate[l].reshape(1, -1),
                    w_a[l].astype(BF16), w_b[l].astype(BF16), w_c[l].astype(BF16),
                    wdw, b_dw[l].reshape(1, -1), g_conv_ln[l].reshape(1, -1),
                    b_conv_ln[l].reshape(1, -1), w_out[l].astype(BF16), skip_ctx=last)
        xs = _ffn(xs, modsel, g_norm2[l].reshape(1, d), w_ffn_in[l].astype(BF16),
                  w_ffn_out[l].astype(BF16), skip_ctx=last)
    return xs
```

```python
import functools
import math

import jax
import jax.numpy as jnp
from jax import lax
from jax.experimental import pallas as pl
from jax.experimental.pallas import tpu as pltpu

F32 = jnp.float32
BF16 = jnp.bfloat16

D_MODEL = 1024
HEAD_DIM = 64
GQA_Q_HEADS = 8
GQA_KV_HEADS = 2
GQA_GROUP = GQA_Q_HEADS // GQA_KV_HEADS
DIFF_HEADS = 4
DIFF_V_DIM = 2 * HEAD_DIM
CONV_CH = 512
CONV_WIDTH = 31
CONV_HALF = CONV_WIDTH // 2
D_FF = 2816
GRID_W = 64
ROPE_THETA = 10000.0
EPS = 1e-6
ATTN_SCALE = HEAD_DIM ** -0.5
N_BRANCH = 3

GQA_Q_W = GQA_Q_HEADS * HEAD_DIM
GQA_KV_W = GQA_KV_HEADS * HEAD_DIM
DIFF_QK_W = DIFF_HEADS * 2 * HEAD_DIM
DIFF_V_W = DIFF_HEADS * DIFF_V_DIM
CONV_IN_W = 2 * CONV_CH
OFF_GQ = 0
OFF_GK = OFF_GQ + GQA_Q_W
OFF_GV = OFF_GK + GQA_KV_W
OFF_DQ = OFF_GV + GQA_KV_W
OFF_DK = OFF_DQ + DIFF_QK_W
OFF_DV = OFF_DK + DIFF_QK_W
OFF_CU = OFF_DV + DIFF_V_W
D_IN = OFF_CU + CONV_IN_W
HEADS_W = OFF_CU

TILE = 256
Q_TILE = 512
LAT_TILE0 = Q_TILE // TILE
HALO = 16
MOD_ROWS = 16
MOD_BLOCK_N = 1536
NEG_BIG = -1e30
LOG2E = math.log2(math.e)
Q_SCALE = ATTN_SCALE * LOG2E
SCORE_BOUND_FAST = 80.0
KEY_BLOCK_CHUNKS = 2
STEPS_PER_TRIP = 4
VMEM_LIMIT = 52 * 1024 * 1024


def _cparams(n_axes):
    return pltpu.CompilerParams(
        dimension_semantics=("arbitrary",) * n_axes, vmem_limit_bytes=VMEM_LIMIT)


def _const_spec(shape):
    nd = len(shape)
    return pl.BlockSpec(shape, lambda *_: (0,) * nd, pipeline_mode=pl.Buffered(1))


def _dot(a, b):
    return jnp.dot(a, b, preferred_element_type=F32)


def _split_bf16(a):
    hi = a.astype(BF16)
    lo = (a - hi.astype(F32)).astype(BF16)
    return hi, lo


def _norm_mod(x, g, shift, scale):
    ms = jnp.mean(x * x, axis=-1, keepdims=True)
    y = x * lax.rsqrt(ms + EPS) * g
    return y * (1.0 + scale) + shift


def _mod_kernel(c_ref, w_ref, b_ref, o_ref):
    c = c_ref[...]
    a = c * jax.nn.sigmoid(c)
    a_hi, a_lo = _split_bf16(a)
    w_hi, w_lo = _split_bf16(w_ref[...])
    o_ref[...] = _dot(a_hi, w_hi) + _dot(a_hi, w_lo) + _dot(a_lo, w_hi) + b_ref[...]


def _modulation(cvec, w_mod, b_mod):
    depth, d, n = w_mod.shape
    return pl.pallas_call(
        _mod_kernel,
        grid=(depth, n // MOD_BLOCK_N),
        in_specs=[
            pl.BlockSpec((MOD_ROWS, d), lambda l, j: (0, 0)),
            pl.BlockSpec((None, d, MOD_BLOCK_N), lambda l, j: (l, 0, j)),
            pl.BlockSpec((None, 1, MOD_BLOCK_N), lambda l, j: (l, 0, j)),
        ],
        out_specs=pl.BlockSpec((None, MOD_ROWS, MOD_BLOCK_N), lambda l, j: (l, 0, j)),
        out_shape=jax.ShapeDtypeStruct((depth, MOD_ROWS, n), F32),
        compiler_params=_cparams(2),
        name="modulation",
    )(cvec, w_mod, b_mod.reshape(depth, 1, n))


def _proj_kernel(x_ref, mod_ref, g1_ref, wt_ref, wcu_ref, gains_ref, cos_ref, sin_ref,
                 qt_ref, k_ref, vt_ref, dqt_ref, dk_ref, dvt_ref, u_ref):
    hb = _norm_mod(x_ref[...], g1_ref[...], mod_ref[0:1, :], mod_ref[1:2, :]).astype(BF16)
    pt = lax.dot_general(wt_ref[...], hb, (((1,), (1,)), ((), ())), preferred_element_type=F32)
    cos = cos_ref[...][None]
    sin = sin_ref[...][None]

    def norm_rope(t, gain_idx, scale):
        n = t.shape[0] // HEAD_DIM
        t3 = t.reshape(n, HEAD_DIM, TILE)
        ms = jnp.mean(t3 * t3, axis=1, keepdims=True)
        y = t3 * lax.rsqrt(ms + EPS) * gains_ref[gain_idx][None]
        q = HEAD_DIM // 4
        swapped = jnp.concatenate(
            [y[:, q:2 * q], y[:, 0:q], y[:, 3 * q:4 * q], y[:, 2 * q:3 * q]], axis=1)
        y = y * cos + swapped * sin
        if scale != 1.0:
            y = y * scale
        return y.reshape(t.shape)

    qt_ref[...] = norm_rope(pt[OFF_GQ:OFF_GK], 0, Q_SCALE).astype(BF16)
    k_ref[...] = norm_rope(pt[OFF_GK:OFF_GV], 1, 1.0).T.astype(BF16)
    vt_ref[...] = pt[OFF_GV:OFF_DQ].astype(BF16)
    dqt_ref[...] = norm_rope(pt[OFF_DQ:OFF_DK], 2, Q_SCALE).astype(BF16)
    dk_ref[...] = norm_rope(pt[OFF_DK:OFF_DV], 3, 1.0).T.astype(BF16)
    dvt_ref[...] = pt[OFF_DV:OFF_CU].astype(BF16)
    cu = _dot(hb, wcu_ref[...])
    u_ref[...] = cu[:, :CONV_CH] * jax.nn.sigmoid(cu[:, CONV_CH:])


def _project(xs, modsel, g1, wt, wcu, gains, cos_t, sin_t):
    b, t, d = xs.shape
    nt = t // TILE
    tile_map = lambda bi, i: (bi, i, 0)
    chunk_map = lambda bi, i: (bi, i, 0, 0)
    lane_map = lambda bi, i: (bi, 0, i)
    return pl.pallas_call(
        _proj_kernel,
        grid=(b, nt),
        in_specs=[
            pl.BlockSpec((None, TILE, d), tile_map),
            pl.BlockSpec((None, None, 6, d), lambda bi, i: (bi, jnp.minimum(i, 1), 0, 0)),
            _const_spec((1, d)),
            _const_spec((HEADS_W, d)),
            _const_spec((d, CONV_IN_W)),
            _const_spec((4, HEAD_DIM, TILE)),
            pl.BlockSpec((HEAD_DIM, TILE), lambda bi, i: (0, i)),
            pl.BlockSpec((HEAD_DIM, TILE), lambda bi, i: (0, i)),
        ],
        out_specs=[
            pl.BlockSpec((None, GQA_Q_W, TILE), lane_map),
            pl.BlockSpec((None, None, TILE, GQA_KV_W), chunk_map),
            pl.BlockSpec((None, None, GQA_KV_W, TILE), chunk_map),
            pl.BlockSpec((None, DIFF_QK_W, TILE), lane_map),
            pl.BlockSpec((None, None, TILE, DIFF_QK_W), chunk_map),
            pl.BlockSpec((None, None, DIFF_V_W, TILE), chunk_map),
            pl.BlockSpec((None, TILE, CONV_CH), tile_map),
        ],
        out_shape=[
            jax.ShapeDtypeStruct((b, GQA_Q_W, t), BF16),
            jax.ShapeDtypeStruct((b, nt, TILE, GQA_KV_W), BF16),
            jax.ShapeDtypeStruct((b, nt, GQA_KV_W, TILE), BF16),
            jax.ShapeDtypeStruct((b, DIFF_QK_W, t), BF16),
            jax.ShapeDtypeStruct((b, nt, TILE, DIFF_QK_W), BF16),
            jax.ShapeDtypeStruct((b, nt, DIFF_V_W, TILE), BF16),
            jax.ShapeDtypeStruct((b, t, CONV_CH), F32),
        ],
        compiler_params=_cparams(2),
        name="project",
    )(xs, modsel, g1, wt, wcu, gains, cos_t, sin_t)


def _stage_padded_q(qt_ref, qpad, n_maps, diff):
    group = pl.program_id(1)
    zeros = jnp.zeros((HEAD_DIM, qt_ref.shape[-1]), BF16)
    for j in range(n_maps):
        qj = qt_ref[HEAD_DIM * j:HEAD_DIM * (j + 1), :]
        if diff:
            halves = [qj, zeros] if j == 0 else [zeros, qj]
        else:
            qf = qj.astype(F32)
            halves = [(qf * (group == h).astype(F32)).astype(BF16) for h in range(GQA_KV_HEADS)]
        qpad[j] = jnp.concatenate(halves, axis=0)


def _finish_attention(o_ref, outs, diff, lam_ref, gsub_ref, lam_init):
    if diff:
        lam_p = lam_ref[...]
        lam = (jnp.exp(jnp.sum(lam_p[0:1] * lam_p[1:2], axis=1, keepdims=True))
               - jnp.exp(jnp.sum(lam_p[2:3] * lam_p[3:4], axis=1, keepdims=True)) + lam_init)
        o = outs[0] - lam * outs[1]
        ms = jnp.mean(o * o, axis=0, keepdims=True)
        o = o * lax.rsqrt(ms + EPS) * gsub_ref[...] * (1.0 - lam_init)
    else:
        o = jnp.concatenate(outs, axis=0)
    o_ref[...] = o.T.astype(BF16)


def _attn_safe_kernel(*refs, n_maps, diff, q_off, lam_init):
    if diff:
        qt_ref, k_ref, vt_ref, lam_ref, gsub_ref, o_ref, qpad, m_s, l_s, acc_s = refs
    else:
        qt_ref, k_ref, vt_ref, o_ref, qpad, m_s, l_s, acc_s = refs
        lam_ref = gsub_ref = None
    q_tile = pl.program_id(2) + q_off
    _stage_padded_q(qt_ref, qpad, n_maps, diff)
    m_s[...] = jnp.full(m_s.shape, NEG_BIG, F32)
    l_s[...] = jnp.zeros(l_s.shape, F32)
    acc_s[...] = jnp.zeros(acc_s.shape, F32)

    n_chunks = jnp.where(q_tile == 0, 1, k_ref.shape[0] - LAT_TILE0 + 1)

    def body(c, carry):
        chunk = jnp.where(c == 0, 0, c + LAT_TILE0 - 1)
        kc = k_ref[chunk]
        vc = vt_ref[chunk]
        for j in range(n_maps):
            s = _dot(kc, qpad[j])
            m_prev = m_s[j]
            m_new = jnp.maximum(m_prev, jnp.max(s, axis=0, keepdims=True))
            alpha = jnp.exp2(m_prev - m_new)
            p = jnp.exp2(s - m_new)
            l_s[j] = alpha * l_s[j] + jnp.sum(p, axis=0, keepdims=True)
            acc_s[j] = alpha * acc_s[j] + _dot(vc, p.astype(BF16))
            m_s[j] = m_new
        return carry

    lax.fori_loop(0, n_chunks, body, 0)
    _finish_attention(o_ref, [acc_s[j] / l_s[j] for j in range(n_maps)],
                      diff, lam_ref, gsub_ref, lam_init)


def _attn_fast_kernel(*refs, n_maps, dv, diff, q_off, lam_init, key_block):
    if diff:
        qt_ref, k_ref, vt_ref, lam_ref, gsub_ref, o_ref, qpad, s_a, s_b, acc_s = refs
    else:
        qt_ref, k_ref, vt_ref, o_ref, qpad, s_a, s_b, acc_s = refs
        lam_ref = gsub_ref = None
    q_tile = pl.program_id(2) + q_off
    chunks_per_block = key_block // TILE
    n_blocks = (k_ref.shape[0] - LAT_TILE0) // chunks_per_block
    _stage_padded_q(qt_ref, qpad, n_maps, diff)

    def v_ext(first_chunk, n):
        v = jnp.concatenate([vt_ref[first_chunk + i] for i in range(n)], axis=1)
        row = lax.broadcasted_iota(jnp.int32, (16, v.shape[1]), 0)
        ones_row = jnp.where(row == 0, 1.0, 0.0).astype(BF16)
        return jnp.concatenate([v, ones_row], axis=0)

    def k_block(i):
        kb = k_ref[pl.ds(LAT_TILE0 + i * chunks_per_block, chunks_per_block)]
        return kb.reshape(key_block, 2 * HEAD_DIM)

    def probs(s):
        return jnp.exp2(s).astype(BF16)

    kc = k_ref[0]
    ve = v_ext(0, 1)
    for j in range(n_maps):
        acc_s[j] = _dot(ve, probs(_dot(kc, qpad[j])))

    def step(src, dst, i, has_next):
        if has_next:
            k_next = k_block(i + 1)
        ve_cur = v_ext(LAT_TILE0 + i * chunks_per_block, chunks_per_block)
        for j in range(n_maps):
            if has_next:
                dst[j] = _dot(k_next, qpad[j])
            acc_s[j] += _dot(ve_cur, probs(src[j]))

    def latent_keys():
        k0 = k_block(0)
        for j in range(n_maps):
            s_a[j] = _dot(k0, qpad[j])

        def steps(first, n, final):
            for e in range(n):
                src, dst = (s_a, s_b) if e % 2 == 0 else (s_b, s_a)
                step(src, dst, first + e, not (final and e == n - 1))

        def body(t, carry):
            steps(STEPS_PER_TRIP * t, STEPS_PER_TRIP, False)
            return carry

        n_trips = (n_blocks - 1) // STEPS_PER_TRIP
        lax.fori_loop(0, n_trips, body, 0)
        steps(n_trips * STEPS_PER_TRIP, n_blocks - n_trips * STEPS_PER_TRIP, True)

    if q_off == 0:
        pl.when(q_tile != 0)(latent_keys)
    else:
        latent_keys()

    outs = []
    for j in range(n_maps):
        acc = acc_s[j]
        outs.append(acc[0:dv] / acc[dv:dv + 1])
    _finish_attention(o_ref, outs, diff, lam_ref, gsub_ref, lam_init)


def _attention_call(qt, k4, vt4, extra, *, diff, skip_ctx, lam_init, fast):
    b, qw, t = qt.shape
    nt = t // TILE
    nq = t // Q_TILE
    if diff:
        n_groups, n_maps, dv = DIFF_HEADS, 2, DIFF_V_DIM
        k_map = lambda bi, g, i: (bi, 0, 0, g)
    else:
        n_groups, n_maps, dv = GQA_KV_HEADS, GQA_GROUP, HEAD_DIM
        k_map = lambda bi, g, i: (bi, 0, 0, 0)
    q_off = 1 if skip_ctx else 0
    out_w = qw // n_groups
    in_specs = [
        pl.BlockSpec((None, n_maps * HEAD_DIM, Q_TILE), lambda bi, g, i: (bi, g, i + q_off)),
        pl.BlockSpec((None, nt, TILE, 2 * HEAD_DIM), k_map, pipeline_mode=pl.Buffered(1)),
        pl.BlockSpec((None, nt, dv, TILE), lambda bi, g, i: (bi, 0, g, 0),
                     pipeline_mode=pl.Buffered(1)),
    ]
    if diff:
        in_specs += [_const_spec((8, 128)), _const_spec((DIFF_V_DIM, Q_TILE))]
    qpad_scratch = pltpu.VMEM((n_maps, 2 * HEAD_DIM, Q_TILE), BF16)
    if fast:
        key_block = _key_block(nt - LAT_TILE0)
        body = functools.partial(_attn_fast_kernel, n_maps=n_maps, dv=dv, diff=diff, q_off=q_off,
                                 lam_init=lam_init, key_block=key_block)
        scratch = [qpad_scratch,
                   pltpu.VMEM((n_maps, key_block, Q_TILE), F32),
                   pltpu.VMEM((n_maps, key_block, Q_TILE), F32),
                   pltpu.VMEM((n_maps, dv + 16, Q_TILE), F32)]
    else:
        body = functools.partial(_attn_safe_kernel, n_maps=n_maps, diff=diff, q_off=q_off,
                                 lam_init=lam_init)
        scratch = [qpad_scratch,
                   pltpu.VMEM((n_maps, 1, Q_TILE), F32),
                   pltpu.VMEM((n_maps, 1, Q_TILE), F32),
                   pltpu.VMEM((n_maps, dv, Q_TILE), F32)]
    kind = ("diff" if diff else "gqa") + ("_attention" if fast else "_attention_safe")
    return pl.pallas_call(
        body,
        grid=(b, n_groups, nq - q_off),
        in_specs=in_specs,
        out_specs=pl.BlockSpec((None, Q_TILE, out_w), lambda bi, g, i: (bi, i + q_off, g)),
        out_shape=jax.ShapeDtypeStruct((b, t, qw), BF16),
        scratch_shapes=scratch,
        compiler_params=_cparams(3),
        name=kind,
    )(qt, k4, vt4, *extra)


def _key_block(n_latent_chunks):
    assert n_latent_chunks % 2 == 0
    chunks = KEY_BLOCK_CHUNKS if n_latent_chunks % (2 * KEY_BLOCK_CHUNKS) == 0 else 1
    return chunks * TILE


def _attention(qt, k4, vt4, g_q, g_k, *, diff, skip_ctx, extra=(), lam_init=0.0):
    bound = (HEAD_DIM * ATTN_SCALE * LOG2E * (1.0 + 2.0 ** -8) ** 2
             * jnp.max(jnp.abs(g_q)) * jnp.max(jnp.abs(g_k)))
    call = functools.partial(_attention_call, diff=diff, skip_ctx=skip_ctx, lam_init=lam_init)
    return lax.cond(bound <= SCORE_BOUND_FAST,
                    lambda *a: call(*a[:3], a[3:], fast=True),
                    lambda *a: call(*a[:3], a[3:], fast=False),
                    qt, k4, vt4, *extra)


def _tile_of(i, skip_ctx):
    if skip_ctx:
        return i + LAT_TILE0
    return jnp.where(i == 0, 0, i + LAT_TILE0 - 1)


def _n_live_tiles(nt, skip_ctx):
    return nt - LAT_TILE0 + (0 if skip_ctx else 1)


def _merge_kernel(x_ref, mod_ref, g1_ref, oa_ref, ob_ref, u_ref, up_ref, un_ref,
                  wgate_ref, bgate_ref, wa_ref, wb_ref, wc_ref, wdw_ref, bdw_ref, gln_ref, bln_ref,
                  wout_ref, o_ref, ubuf, conv_s, *, skip_ctx, last_tile):
    x = x_ref[...]
    hb = _norm_mod(x, g1_ref[...], mod_ref[0:1, :], mod_ref[1:2, :]).astype(BF16)
    gates = jax.nn.sigmoid(_dot(hb, wgate_ref[...]) + bgate_ref[...])
    y_a = _dot(oa_ref[...], wa_ref[...])
    y_b = _dot(ob_ref[...], wb_ref[...])

    tile = _tile_of(pl.program_id(1), skip_ctx)
    has_prev = jnp.logical_and(tile != 0, tile != LAT_TILE0).astype(F32)
    has_next = jnp.logical_and(tile != 0, tile != last_tile).astype(F32)
    ubuf[0:HALO, :] = up_ref[...] * has_prev
    ubuf[HALO:HALO + TILE, :] = u_ref[...]
    ubuf[HALO + TILE:, :] = un_ref[...] * has_next
    rows = 32
    for r in range(0, TILE, rows):
        acc = jnp.zeros((rows, CONV_CH), F32) + bdw_ref[...]
        for j in range(CONV_WIDTH):
            start = r + HALO - CONV_HALF + j
            acc = acc + ubuf[start:start + rows, :] * wdw_ref[j:j + 1, :]
        conv_s[r:r + rows, :] = acc
    v = conv_s[...]
    mu = jnp.mean(v, axis=-1, keepdims=True)
    vc = v - mu
    var = jnp.mean(vc * vc, axis=-1, keepdims=True)
    v = vc * lax.rsqrt(var + EPS) * gln_ref[...] + bln_ref[...]
    v = v * jax.nn.sigmoid(v)
    y_c = _dot(v.astype(BF16), wc_ref[...])

    merged = (gates[:, 0:D_MODEL] * y_a + gates[:, D_MODEL:2 * D_MODEL] * y_b
              + gates[:, 2 * D_MODEL:] * y_c)
    y = _dot(merged.astype(BF16), wout_ref[...])
    o_ref[...] = x + mod_ref[2:3, :] * y


def _merge(xs, modsel, g1, oa, ob, u, wgate, bgate, wa, wb, wc, wdw, bdw, gln, bln, wout, *, skip_ctx):
    b, t, d = xs.shape
    nt = t // TILE
    hb_per_tile = TILE // HALO
    n_halo_blocks = t // HALO
    tile_of = functools.partial(_tile_of, skip_ctx=skip_ctx)
    tile_map = lambda bi, i: (bi, tile_of(i), 0)
    return pl.pallas_call(
        functools.partial(_merge_kernel, skip_ctx=skip_ctx, last_tile=nt - 1),
        grid=(b, _n_live_tiles(nt, skip_ctx)),
        in_specs=[
            pl.BlockSpec((None, TILE, d), tile_map),
            pl.BlockSpec((None, None, 6, d), lambda bi, i: (bi, jnp.minimum(tile_of(i), 1), 0, 0)),
            _const_spec((1, d)),
            pl.BlockSpec((None, TILE, GQA_Q_W), tile_map),
            pl.BlockSpec((None, TILE, DIFF_V_W), tile_map),
            pl.BlockSpec((None, TILE, CONV_CH), tile_map),
            pl.BlockSpec((None, HALO, CONV_CH),
                         lambda bi, i: (bi, jnp.maximum(tile_of(i) * hb_per_tile - 1, 0), 0)),
            pl.BlockSpec((None, HALO, CONV_CH),
                         lambda bi, i: (bi, jnp.minimum((tile_of(i) + 1) * hb_per_tile,
                                                        n_halo_blocks - 1), 0)),
            _const_spec((d, N_BRANCH * d)),
            _const_spec((1, N_BRANCH * d)),
            _const_spec((GQA_Q_W, d)),
            _const_spec((DIFF_V_W, d)),
            _const_spec((CONV_CH, d)),
            _const_spec((CONV_WIDTH + 1, CONV_CH)),
            _const_spec((1, CONV_CH)),
            _const_spec((1, CONV_CH)),
            _const_spec((1, CONV_CH)),
            _const_spec((d, d)),
        ],
        out_specs=pl.BlockSpec((None, TILE, d), tile_map),
        out_shape=jax.ShapeDtypeStruct(xs.shape, F32),
        scratch_shapes=[
            pltpu.VMEM((TILE + 2 * HALO, CONV_CH), F32),
            pltpu.VMEM((TILE, CONV_CH), F32),
        ],
        input_output_aliases={0: 0},
        compiler_params=_cparams(2),
        name="merge",
    )(xs, modsel, g1, oa, ob, u, u, u, wgate, bgate, wa, wb, wc, wdw, bdw, gln, bln, wout)


def _ffn_kernel(x_ref, mod_ref, g2_ref, win_ref, wout_ref, o_ref):
    x = x_ref[...]
    hb = _norm_mod(x, g2_ref[...], mod_ref[3:4, :], mod_ref[4:5, :]).astype(BF16)
    gate = _dot(hb, win_ref[:, :D_FF])
    up = _dot(hb, win_ref[:, D_FF:])
    act = (gate * jax.nn.sigmoid(gate) * up).astype(BF16)
    o_ref[...] = x + mod_ref[5:6, :] * _dot(act, wout_ref[...])


def _ffn(xs, modsel, g2, win, wout, *, skip_ctx):
    b, t, d = xs.shape
    nt = t // TILE
    tile_of = functools.partial(_tile_of, skip_ctx=skip_ctx)
    kwargs = {}
    if skip_ctx:
        out_shape = jax.ShapeDtypeStruct((b, t - LAT_TILE0 * TILE, d), F32)
        out_map = lambda bi, i: (bi, i, 0)
    else:
        out_shape = jax.ShapeDtypeStruct(xs.shape, F32)
        out_map = lambda bi, i: (bi, tile_of(i), 0)
        kwargs["input_output_aliases"] = {0: 0}
    return pl.pallas_call(
        _ffn_kernel,
        grid=(b, _n_live_tiles(nt, skip_ctx)),
        in_specs=[
            pl.BlockSpec((None, TILE, d), lambda bi, i: (bi, tile_of(i), 0)),
            pl.BlockSpec((None, None, 6, d), lambda bi, i: (bi, jnp.minimum(tile_of(i), 1), 0, 0)),
            _const_spec((1, d)),
            _const_spec((d, 2 * D_FF)),
            _const_spec((D_FF, d)),
        ],
        out_specs=pl.BlockSpec((None, TILE, d), out_map),
        out_shape=out_shape,
        compiler_params=_cparams(2),
        name="swiglu",
        **kwargs,
    )(xs, modsel, g2, win, wout)


def _rope_tables_t(n_ctx, n_lat):
    n_freq = HEAD_DIM // 4
    pos = jnp.arange(n_lat, dtype=jnp.int32)
    row = (pos // GRID_W).astype(F32)
    col = (pos % GRID_W).astype(F32)
    inv = ROPE_THETA ** (-jnp.arange(n_freq, dtype=F32) / n_freq)
    ang_r = inv[:, None] * row[None, :]
    ang_c = inv[:, None] * col[None, :]
    cos = jnp.concatenate([jnp.cos(ang_r)] * 2 + [jnp.cos(ang_c)] * 2, axis=0)
    sin = jnp.concatenate([-jnp.sin(ang_r), jnp.sin(ang_r), -jnp.sin(ang_c), jnp.sin(ang_c)], axis=0)
    cos = jnp.concatenate([jnp.ones((HEAD_DIM, n_ctx), F32), cos], axis=1)
    sin = jnp.concatenate([jnp.zeros((HEAD_DIM, n_ctx), F32), sin], axis=1)
    return cos, sin


def kernel(x, c, ctx, c_ctx, w_mod, b_mod, g_norm1, w_in, w_gate, b_gate, g_q_gqa, g_k_gqa, g_q_diff, g_k_diff, lambda_q1, lambda_k1, lambda_q2, lambda_k2, g_subln, w_dw, b_dw, g_conv_ln, b_conv_ln, w_a, w_b, w_c, w_out, g_norm2, w_ffn_in, w_ffn_out):
    batch, n_lat, d = x.shape
    n_ctx = ctx.shape[1]
    depth = w_mod.shape[0]
    assert n_ctx == TILE and n_lat % Q_TILE == 0 and d == D_MODEL and batch + 1 <= MOD_ROWS
    n_pad = Q_TILE - n_ctx

    cvec = jnp.concatenate(
        [c, c_ctx[None, :], jnp.zeros((MOD_ROWS - batch - 1, d), F32)], axis=0)
    mods = _modulation(cvec, w_mod, b_mod)
    cos_t, sin_t = _rope_tables_t(n_ctx + n_pad, n_lat)
    xs = jnp.concatenate([ctx, jnp.zeros((batch, n_pad, d), F32), x], axis=1)

    for l in range(depth):
        last = l == depth - 1
        lam_init = 0.8 - 0.6 * math.exp(-0.3 * l)
        lat_mod = mods[l, :batch].reshape(batch, 1, 6, d)
        ctx_mod = jnp.broadcast_to(mods[l, batch].reshape(1, 1, 6, d), (batch, 1, 6, d))
        modsel = jnp.concatenate([ctx_mod, lat_mod], axis=1)
        gains = jnp.broadcast_to(
            jnp.stack([g_q_gqa[l], g_k_gqa[l], g_q_diff[l], g_k_diff[l]])[:, :, None],
            (4, HEAD_DIM, TILE))
        lam_rows = jnp.zeros((8, 128), F32).at[0:4, 0:HEAD_DIM].set(
            jnp.stack([lambda_q1[l], lambda_k1[l], lambda_q2[l], lambda_k2[l]]))
        gsub = jnp.broadcast_to(g_subln[l][:, None], (DIFF_V_DIM, Q_TILE))
        g1 = g_norm1[l].reshape(1, d)
        wt = w_in[l, :, :HEADS_W].T.astype(BF16)
        wcu = w_in[l, :, HEADS_W:].astype(BF16)
        wdw = jnp.concatenate([w_dw[l], jnp.zeros((1, CONV_CH), F32)], axis=0)

        qt, k4, vt4, dqt, dk4, dvt4, u = _project(xs, modsel, g1, wt, wcu, gains, cos_t, sin_t)
        o_gqa = _attention(qt, k4, vt4, g_q_gqa[l], g_k_gqa[l], diff=False, skip_ctx=last)
        o_diff = _attention(dqt, dk4, dvt4, g_q_diff[l], g_k_diff[l], diff=True, skip_ctx=last,
                            extra=(lam_rows, gsub), lam_init=lam_init)
        xs = _merge(xs, modsel, g1, o_gqa, o_diff, u,
                    w_gate[l].astype(BF16), b_gate[l].reshape(1, -1),
                    w_a[l].astype(BF16), w_b[l].astype(BF16), w_c[l].astype(BF16),
                    wdw, b_dw[l].reshape(1, -1), g_conv_ln[l].reshape(1, -1),
                    b_conv_ln[l].reshape(1, -1), w_out[l].astype(BF16), skip_ctx=last)
        xs = _ffn(xs, modsel, g_norm2[l].reshape(1, d), w_ffn_in[l].astype(BF16),
                  w_ffn_out[l].astype(BF16), skip_ctx=last)
    return xs
```

```python
import functools
import math

import jax
import jax.numpy as jnp
from jax import lax
from jax.experimental import pallas as pl
from jax.experimental.pallas import tpu as pltpu

F32 = jnp.float32
BF16 = jnp.bfloat16

D_MODEL = 1024
HEAD_DIM = 64
GQA_Q_HEADS = 8
GQA_KV_HEADS = 2
GQA_GROUP = GQA_Q_HEADS // GQA_KV_HEADS
DIFF_HEADS = 4
DIFF_V_DIM = 2 * HEAD_DIM
CONV_CH = 512
CONV_WIDTH = 31
CONV_HALF = CONV_WIDTH // 2
D_FF = 2816
GRID_W = 64
ROPE_THETA = 10000.0
EPS = 1e-6
ATTN_SCALE = HEAD_DIM ** -0.5
N_BRANCH = 3

GQA_Q_W = GQA_Q_HEADS * HEAD_DIM
GQA_KV_W = GQA_KV_HEADS * HEAD_DIM
DIFF_QK_W = DIFF_HEADS * 2 * HEAD_DIM
DIFF_V_W = DIFF_HEADS * DIFF_V_DIM
CONV_IN_W = 2 * CONV_CH
OFF_GQ = 0
OFF_GK = OFF_GQ + GQA_Q_W
OFF_GV = OFF_GK + GQA_KV_W
OFF_DQ = OFF_GV + GQA_KV_W
OFF_DK = OFF_DQ + DIFF_QK_W
OFF_DV = OFF_DK + DIFF_QK_W
OFF_CU = OFF_DV + DIFF_V_W
D_IN = OFF_CU + CONV_IN_W
HEADS_W = OFF_CU

TILE = 256
Q_TILE = 512
LAT_TILE0 = Q_TILE // TILE
HALO = 16
MOD_ROWS = 16
MOD_BLOCK_N = 1536
NEG_BIG = -1e30
LOG2E = math.log2(math.e)
Q_SCALE = ATTN_SCALE * LOG2E
SCORE_BOUND_FAST = 80.0
KEY_BLOCK_CHUNKS = 2
STEPS_PER_TRIP = 4
VMEM_LIMIT = 52 * 1024 * 1024


def _cparams(n_axes):
    return pltpu.CompilerParams(
        dimension_semantics=("arbitrary",) * n_axes, vmem_limit_bytes=VMEM_LIMIT)


def _const_spec(shape):
    nd = len(shape)
    return pl.BlockSpec(shape, lambda *_: (0,) * nd, pipeline_mode=pl.Buffered(1))


def _dot(a, b):
    return jnp.dot(a, b, preferred_element_type=F32)


def _split_bf16(a):
    hi = a.astype(BF16)
    lo = (a - hi.astype(F32)).astype(BF16)
    return hi, lo


def _norm_mod(x, g, shift, scale):
    ms = jnp.mean(x * x, axis=-1, keepdims=True)
    y = x * lax.rsqrt(ms + EPS) * g
    return y * (1.0 + scale) + shift


def _mod_kernel(c_ref, w_ref, b_ref, o_ref):
    c = c_ref[...]
    a = c * jax.nn.sigmoid(c)
    a_hi, a_lo = _split_bf16(a)
    w_hi, w_lo = _split_bf16(w_ref[...])
    o_ref[...] = _dot(a_hi, w_hi) + _dot(a_hi, w_lo) + _dot(a_lo, w_hi) + b_ref[...]


def _modulation(cvec, w_mod, b_mod):
    depth, d, n = w_mod.shape
    return pl.pallas_call(
        _mod_kernel,
        grid=(depth, n // MOD_BLOCK_N),
        in_specs=[
            pl.BlockSpec((MOD_ROWS, d), lambda l, j: (0, 0)),
            pl.BlockSpec((None, d, MOD_BLOCK_N), lambda l, j: (l, 0, j)),
            pl.BlockSpec((None, 1, MOD_BLOCK_N), lambda l, j: (l, 0, j)),
        ],
        out_specs=pl.BlockSpec((None, MOD_ROWS, MOD_BLOCK_N), lambda l, j: (l, 0, j)),
        out_shape=jax.ShapeDtypeStruct((depth, MOD_ROWS, n), F32),
        compiler_params=_cparams(2),
        name="modulation",
    )(cvec, w_mod, b_mod.reshape(depth, 1, n))


def _proj_kernel(x_ref, mod_ref, g1_ref, wt_ref, wcu_ref, gains_ref, cos_ref, sin_ref,
                 qt_ref, k_ref, vt_ref, dqt_ref, dk_ref, dvt_ref, u_ref):
    hb = _norm_mod(x_ref[...], g1_ref[...], mod_ref[0:1, :], mod_ref[1:2, :]).astype(BF16)
    pt = lax.dot_general(wt_ref[...], hb, (((1,), (1,)), ((), ())), preferred_element_type=F32)
    cos = cos_ref[...][None]
    sin = sin_ref[...][None]

    def norm_rope(t, gain_idx, scale):
        n = t.shape[0] // HEAD_DIM
        t3 = t.reshape(n, HEAD_DIM, TILE)
        ms = jnp.mean(t3 * t3, axis=1, keepdims=True)
        y = t3 * lax.rsqrt(ms + EPS) * gains_ref[gain_idx][None]
        q = HEAD_DIM // 4
        swapped = jnp.concatenate(
            [y[:, q:2 * q], y[:, 0:q], y[:, 3 * q:4 * q], y[:, 2 * q:3 * q]], axis=1)
        y = y * cos + swapped * sin
        if scale != 1.0:
            y = y * scale
        return y.reshape(t.shape)

    qt_ref[...] = norm_rope(pt[OFF_GQ:OFF_GK], 0, Q_SCALE).astype(BF16)
    k_ref[...] = norm_rope(pt[OFF_GK:OFF_GV], 1, 1.0).T.astype(BF16)
    vt_ref[...] = pt[OFF_GV:OFF_DQ].astype(BF16)
    dqt_ref[...] = norm_rope(pt[OFF_DQ:OFF_DK], 2, Q_SCALE).astype(BF16)
    dk_ref[...] = norm_rope(pt[OFF_DK:OFF_DV], 3, 1.0).T.astype(BF16)
    dvt_ref[...] = pt[OFF_DV:OFF_CU].astype(BF16)
    cu = _dot(hb, wcu_ref[...])
    u_ref[...] = cu[:, :CONV_CH] * jax.nn.sigmoid(cu[:, CONV_CH:])


def _project(xs, modsel, g1, wt, wcu, gains, cos_t, sin_t):
    b, t, d = xs.shape
    nt = t // TILE
    tile_map = lambda bi, i: (bi, i, 0)
    chunk_map = lambda bi, i: (bi, i, 0, 0)
    lane_map = lambda bi, i: (bi, 0, i)
    return pl.pallas_call(
        _proj_kernel,
        grid=(b, nt),
        in_specs=[
            pl.BlockSpec((None, TILE, d), tile_map),
            pl.BlockSpec((None, None, 6, d), lambda bi, i: (bi, jnp.minimum(i, 1), 0, 0)),
            _const_spec((1, d)),
            _const_spec((HEADS_W, d)),
            _const_spec((d, CONV_IN_W)),
            _const_spec((4, HEAD_DIM, TILE)),
            pl.BlockSpec((HEAD_DIM, TILE), lambda bi, i: (0, i)),
            pl.BlockSpec((HEAD_DIM, TILE), lambda bi, i: (0, i)),
        ],
        out_specs=[
            pl.BlockSpec((None, GQA_Q_W, TILE), lane_map),
            pl.BlockSpec((None, None, TILE, GQA_KV_W), chunk_map),
            pl.BlockSpec((None, None, GQA_KV_W, TILE), chunk_map),
            pl.BlockSpec((None, DIFF_QK_W, TILE), lane_map),
            pl.BlockSpec((None, None, TILE, DIFF_QK_W), chunk_map),
            pl.BlockSpec((None, None, DIFF_V_W, TILE), chunk_map),
            pl.BlockSpec((None, TILE, CONV_CH), tile_map),
        ],
        out_shape=[
            jax.ShapeDtypeStruct((b, GQA_Q_W, t), BF16),
            jax.ShapeDtypeStruct((b, nt, TILE, GQA_KV_W), BF16),
            jax.ShapeDtypeStruct((b, nt, GQA_KV_W, TILE), BF16),
            jax.ShapeDtypeStruct((b, DIFF_QK_W, t), BF16),
            jax.ShapeDtypeStruct((b, nt, TILE, DIFF_QK_W), BF16),
            jax.ShapeDtypeStruct((b, nt, DIFF_V_W, TILE), BF16),
            jax.ShapeDtypeStruct((b, t, CONV_CH), F32),
        ],
        compiler_params=_cparams(2),
        name="project",
    )(xs, modsel, g1, wt, wcu, gains, cos_t, sin_t)


def _stage_padded_q(qt_ref, qpad, n_maps, diff):
    group = pl.program_id(1)
    zeros = jnp.zeros((HEAD_DIM, qt_ref.shape[-1]), BF16)
    for j in range(n_maps):
        qj = qt_ref[HEAD_DIM * j:HEAD_DIM * (j + 1), :]
        if diff:
            halves = [qj, zeros] if j == 0 else [zeros, qj]
        else:
            qf = qj.astype(F32)
            halves = [(qf * (group == h).astype(F32)).astype(BF16) for h in range(GQA_KV_HEADS)]
        qpad[j] = jnp.concatenate(halves, axis=0)


def _finish_attention(o_ref, outs, diff, lam_ref, gsub_ref, lam_init):
    if diff:
        lam_p = lam_ref[...]
        lam = (jnp.exp(jnp.sum(lam_p[0:1] * lam_p[1:2], axis=1, keepdims=True))
               - jnp.exp(jnp.sum(lam_p[2:3] * lam_p[3:4], axis=1, keepdims=True)) + lam_init)
        o = outs[0] - lam * outs[1]
        ms = jnp.mean(o * o, axis=0, keepdims=True)
        o = o * lax.rsqrt(ms + EPS) * gsub_ref[...] * (1.0 - lam_init)
    else:
        o = jnp.concatenate(outs, axis=0)
    o_ref[...] = o.T.astype(BF16)


def _attn_safe_kernel(*refs, n_maps, diff, q_off, lam_init):
    if diff:
        qt_ref, k_ref, vt_ref, lam_ref, gsub_ref, o_ref, qpad, m_s, l_s, acc_s = refs
    else:
        qt_ref, k_ref, vt_ref, o_ref, qpad, m_s, l_s, acc_s = refs
        lam_ref = gsub_ref = None
    q_tile = pl.program_id(2) + q_off
    _stage_padded_q(qt_ref, qpad, n_maps, diff)
    m_s[...] = jnp.full(m_s.shape, NEG_BIG, F32)
    l_s[...] = jnp.zeros(l_s.shape, F32)
    acc_s[...] = jnp.zeros(acc_s.shape, F32)

    n_chunks = jnp.where(q_tile == 0, 1, k_ref.shape[0] - LAT_TILE0 + 1)

    def body(c, carry):
        chunk = jnp.where(c == 0, 0, c + LAT_TILE0 - 1)
        kc = k_ref[chunk]
        vc = vt_ref[chunk]
        for j in range(n_maps):
            s = _dot(kc, qpad[j])
            m_prev = m_s[j]
            m_new = jnp.maximum(m_prev, jnp.max(s, axis=0, keepdims=True))
            alpha = jnp.exp2(m_prev - m_new)
            p = jnp.exp2(s - m_new)
            l_s[j] = alpha * l_s[j] + jnp.sum(p, axis=0, keepdims=True)
            acc_s[j] = alpha * acc_s[j] + _dot(vc, p.astype(BF16))
            m_s[j] = m_new
        return carry

    lax.fori_loop(0, n_chunks, body, 0)
    _finish_attention(o_ref, [acc_s[j] / l_s[j] for j in range(n_maps)],
                      diff, lam_ref, gsub_ref, lam_init)


def _attn_fast_kernel(*refs, n_maps, dv, diff, q_off, lam_init, key_block):
    if diff:
        qt_ref, k_ref, vt_ref, lam_ref, gsub_ref, o_ref, qpad, s_a, s_b, acc_s = refs
    else:
        qt_ref, k_ref, vt_ref, o_ref, qpad, s_a, s_b, acc_s = refs
        lam_ref = gsub_ref = None
    q_tile = pl.program_id(2) + q_off
    chunks_per_block = key_block // TILE
    n_blocks = (k_ref.shape[0] - LAT_TILE0) // chunks_per_block
    _stage_padded_q(qt_ref, qpad, n_maps, diff)

    def k_block(i):
        kb = k_ref[pl.ds(LAT_TILE0 + i * chunks_per_block, chunks_per_block)]
        return kb.reshape(key_block, 2 * HEAD_DIM)

    def v_ext(chunk):
        v = vt_ref[chunk]
        row = lax.broadcasted_iota(jnp.int32, (16, v.shape[1]), 0)
        return jnp.concatenate([v, jnp.where(row == 0, 1.0, 0.0).astype(BF16)], axis=0)

    def probs(s):
        return jnp.exp2(s).astype(BF16)

    kc = k_ref[0]
    ve = v_ext(0)
    for j in range(n_maps):
        acc_s[j] = _dot(ve, probs(_dot(kc, qpad[j])))

    def step(src, dst, i, has_next):
        first = LAT_TILE0 + i * chunks_per_block
        for j in range(n_maps):
            acc = acc_s[j]
            for c in range(chunks_per_block):
                rows = slice(c * TILE, (c + 1) * TILE)
                if has_next:
                    dst[j, rows, :] = _dot(k_ref[first + chunks_per_block + c], qpad[j])
                acc = acc + _dot(v_ext(first + c), probs(src[j, rows, :]))
            acc_s[j] = acc

    def latent_keys():
        k0 = k_block(0)
        for j in range(n_maps):
            s_a[j] = _dot(k0, qpad[j])

        def steps(first, n, final):
            for e in range(n):
                src, dst = (s_a, s_b) if e % 2 == 0 else (s_b, s_a)
                step(src, dst, first + e, not (final and e == n - 1))

        def body(t, carry):
            steps(STEPS_PER_TRIP * t, STEPS_PER_TRIP, False)
            return carry

        n_trips = (n_blocks - 1) // STEPS_PER_TRIP
        lax.fori_loop(0, n_trips, body, 0)
        steps(n_trips * STEPS_PER_TRIP, n_blocks - n_trips * STEPS_PER_TRIP, True)

    if q_off == 0:
        pl.when(q_tile != 0)(latent_keys)
    else:
        latent_keys()

    outs = []
    for j in range(n_maps):
        acc = acc_s[j]
        outs.append(acc[0:dv] / acc[dv:dv + 1])
    _finish_attention(o_ref, outs, diff, lam_ref, gsub_ref, lam_init)


def _attention_call(qt, k4, vt4, extra, *, diff, skip_ctx, lam_init, fast):
    b, qw, t = qt.shape
    nt = t // TILE
    nq = t // Q_TILE
    if diff:
        n_groups, n_maps, dv = DIFF_HEADS, 2, DIFF_V_DIM
        k_map = lambda bi, g, i: (bi, 0, 0, g)
    else:
        n_groups, n_maps, dv = GQA_KV_HEADS, GQA_GROUP, HEAD_DIM
        k_map = lambda bi, g, i: (bi, 0, 0, 0)
    q_off = 1 if skip_ctx else 0
    out_w = qw // n_groups
    in_specs = [
        pl.BlockSpec((None, n_maps * HEAD_DIM, Q_TILE), lambda bi, g, i: (bi, g, i + q_off)),
        pl.BlockSpec((None, nt, TILE, 2 * HEAD_DIM), k_map, pipeline_mode=pl.Buffered(1)),
        pl.BlockSpec((None, nt, dv, TILE), lambda bi, g, i: (bi, 0, g, 0),
                     pipeline_mode=pl.Buffered(1)),
    ]
    if diff:
        in_specs += [_const_spec((8, 128)), _const_spec((DIFF_V_DIM, Q_TILE))]
    qpad_scratch = pltpu.VMEM((n_maps, 2 * HEAD_DIM, Q_TILE), BF16)
    if fast:
        key_block = _key_block(nt - LAT_TILE0)
        body = functools.partial(_attn_fast_kernel, n_maps=n_maps, dv=dv, diff=diff, q_off=q_off,
                                 lam_init=lam_init, key_block=key_block)
        scratch = [qpad_scratch,
                   pltpu.VMEM((n_maps, key_block, Q_TILE), F32),
                   pltpu.VMEM((n_maps, key_block, Q_TILE), F32),
                   pltpu.VMEM((n_maps, dv + 16, Q_TILE), F32)]
    else:
        body = functools.partial(_attn_safe_kernel, n_maps=n_maps, diff=diff, q_off=q_off,
                                 lam_init=lam_init)
        scratch = [qpad_scratch,
                   pltpu.VMEM((n_maps, 1, Q_TILE), F32),
                   pltpu.VMEM((n_maps, 1, Q_TILE), F32),
                   pltpu.VMEM((n_maps, dv, Q_TILE), F32)]
    kind = ("diff" if diff else "gqa") + ("_attention" if fast else "_attention_safe")
    return pl.pallas_call(
        body,
        grid=(b, n_groups, nq - q_off),
        in_specs=in_specs,
        out_specs=pl.BlockSpec((None, Q_TILE, out_w), lambda bi, g, i: (bi, i + q_off, g)),
        out_shape=jax.ShapeDtypeStruct((b, t, qw), BF16),
        scratch_shapes=scratch,
        compiler_params=_cparams(3),
        name=kind,
    )(qt, k4, vt4, *extra)


def _key_block(n_latent_chunks):
    assert n_latent_chunks % 2 == 0
    chunks = KEY_BLOCK_CHUNKS if n_latent_chunks % (2 * KEY_BLOCK_CHUNKS) == 0 else 1
    return chunks * TILE


def _attention(qt, k4, vt4, g_q, g_k, *, diff, skip_ctx, extra=(), lam_init=0.0):
    bound = (HEAD_DIM * ATTN_SCALE * LOG2E * (1.0 + 2.0 ** -8) ** 2
             * jnp.max(jnp.abs(g_q)) * jnp.max(jnp.abs(g_k)))
    call = functools.partial(_attention_call, diff=diff, skip_ctx=skip_ctx, lam_init=lam_init)
    return lax.cond(bound <= SCORE_BOUND_FAST,
                    lambda *a: call(*a[:3], a[3:], fast=True),
                    lambda *a: call(*a[:3], a[3:], fast=False),
                    qt, k4, vt4, *extra)


def _tile_of(i, skip_ctx):
    if skip_ctx:
        return i + LAT_TILE0
    return jnp.where(i == 0, 0, i + LAT_TILE0 - 1)


def _n_live_tiles(nt, skip_ctx):
    return nt - LAT_TILE0 + (0 if skip_ctx else 1)


def _merge_kernel(x_ref, mod_ref, g1_ref, oa_ref, ob_ref, u_ref, up_ref, un_ref,
                  wgate_ref, bgate_ref, wa_ref, wb_ref, wc_ref, wdw_ref, bdw_ref, gln_ref, bln_ref,
                  wout_ref, o_ref, ubuf, conv_s, *, skip_ctx, last_tile):
    x = x_ref[...]
    hb = _norm_mod(x, g1_ref[...], mod_ref[0:1, :], mod_ref[1:2, :]).astype(BF16)
    gates = jax.nn.sigmoid(_dot(hb, wgate_ref[...]) + bgate_ref[...])
    y_a = _dot(oa_ref[...], wa_ref[...])
    y_b = _dot(ob_ref[...], wb_ref[...])

    tile = _tile_of(pl.program_id(1), skip_ctx)
    has_prev = jnp.logical_and(tile != 0, tile != LAT_TILE0).astype(F32)
    has_next = jnp.logical_and(tile != 0, tile != last_tile).astype(F32)
    ubuf[0:HALO, :] = up_ref[...] * has_prev
    ubuf[HALO:HALO + TILE, :] = u_ref[...]
    ubuf[HALO + TILE:, :] = un_ref[...] * has_next
    rows = 32
    for r in range(0, TILE, rows):
        acc = jnp.zeros((rows, CONV_CH), F32) + bdw_ref[...]
        for j in range(CONV_WIDTH):
            start = r + HALO - CONV_HALF + j
            acc = acc + ubuf[start:start + rows, :] * wdw_ref[j:j + 1, :]
        conv_s[r:r + rows, :] = acc
    v = conv_s[...]
    mu = jnp.mean(v, axis=-1, keepdims=True)
    vc = v - mu
    var = jnp.mean(vc * vc, axis=-1, keepdims=True)
    v = vc * lax.rsqrt(var + EPS) * gln_ref[...] + bln_ref[...]
    v = v * jax.nn.sigmoid(v)
    y_c = _dot(v.astype(BF16), wc_ref[...])

    merged = (gates[:, 0:D_MODEL] * y_a + gates[:, D_MODEL:2 * D_MODEL] * y_b
              + gates[:, 2 * D_MODEL:] * y_c)
    y = _dot(merged.astype(BF16), wout_ref[...])
    o_ref[...] = x + mod_ref[2:3, :] * y


def _merge(xs, modsel, g1, oa, ob, u, wgate, bgate, wa, wb, wc, wdw, bdw, gln, bln, wout, *, skip_ctx):
    b, t, d = xs.shape
    nt = t // TILE
    hb_per_tile = TILE // HALO
    n_halo_blocks = t // HALO
    tile_of = functools.partial(_tile_of, skip_ctx=skip_ctx)
    tile_map = lambda bi, i: (bi, tile_of(i), 0)
    return pl.pallas_call(
        functools.partial(_merge_kernel, skip_ctx=skip_ctx, last_tile=nt - 1),
        grid=(b, _n_live_tiles(nt, skip_ctx)),
        in_specs=[
            pl.BlockSpec((None, TILE, d), tile_map),
            pl.BlockSpec((None, None, 6, d), lambda bi, i: (bi, jnp.minimum(tile_of(i), 1), 0, 0)),
            _const_spec((1, d)),
            pl.BlockSpec((None, TILE, GQA_Q_W), tile_map),
            pl.BlockSpec((None, TILE, DIFF_V_W), tile_map),
            pl.BlockSpec((None, TILE, CONV_CH), tile_map),
            pl.BlockSpec((None, HALO, CONV_CH),
                         lambda bi, i: (bi, jnp.maximum(tile_of(i) * hb_per_tile - 1, 0), 0)),
            pl.BlockSpec((None, HALO, CONV_CH),
                         lambda bi, i: (bi, jnp.minimum((tile_of(i) + 1) * hb_per_tile,
                                                        n_halo_blocks - 1), 0)),
            _const_spec((d, N_BRANCH * d)),
            _const_spec((1, N_BRANCH * d)),
            _const_spec((GQA_Q_W, d)),
            _const_spec((DIFF_V_W, d)),
            _const_spec((CONV_CH, d)),
            _const_spec((CONV_WIDTH + 1, CONV_CH)),
            _const_spec((1, CONV_CH)),
            _const_spec((1, CONV_CH)),
            _const_spec((1, CONV_CH)),
            _const_spec((d, d)),
        ],
        out_specs=pl.BlockSpec((None, TILE, d), tile_map),
        out_shape=jax.ShapeDtypeStruct(xs.shape, F32),
        scratch_shapes=[
            pltpu.VMEM((TILE + 2 * HALO, CONV_CH), F32),
            pltpu.VMEM((TILE, CONV_CH), F32),
        ],
        input_output_aliases={0: 0},
        compiler_params=_cparams(2),
        name="merge",
    )(xs, modsel, g1, oa, ob, u, u, u, wgate, bgate, wa, wb, wc, wdw, bdw, gln, bln, wout)


def _ffn_kernel(x_ref, mod_ref, g2_ref, win_ref, wout_ref, o_ref):
    x = x_ref[...]
    hb = _norm_mod(x, g2_ref[...], mod_ref[3:4, :], mod_ref[4:5, :]).astype(BF16)
    gate = _dot(hb, win_ref[:, :D_FF])
    up = _dot(hb, win_ref[:, D_FF:])
    act = (gate * jax.nn.sigmoid(gate) * up).astype(BF16)
    o_ref[...] = x + mod_ref[5:6, :] * _dot(act, wout_ref[...])


def _ffn(xs, modsel, g2, win, wout, *, skip_ctx):
    b, t, d = xs.shape
    nt = t // TILE
    tile_of = functools.partial(_tile_of, skip_ctx=skip_ctx)
    kwargs = {}
    if skip_ctx:
        out_shape = jax.ShapeDtypeStruct((b, t - LAT_TILE0 * TILE, d), F32)
        out_map = lambda bi, i: (bi, i, 0)
    else:
        out_shape = jax.ShapeDtypeStruct(xs.shape, F32)
        out_map = lambda bi, i: (bi, tile_of(i), 0)
        kwargs["input_output_aliases"] = {0: 0}
    return pl.pallas_call(
        _ffn_kernel,
        grid=(b, _n_live_tiles(nt, skip_ctx)),
        in_specs=[
            pl.BlockSpec((None, TILE, d), lambda bi, i: (bi, tile_of(i), 0)),
            pl.BlockSpec((None, None, 6, d), lambda bi, i: (bi, jnp.minimum(tile_of(i), 1), 0, 0)),
            _const_spec((1, d)),
            _const_spec((d, 2 * D_FF)),
            _const_spec((D_FF, d)),
        ],
        out_specs=pl.BlockSpec((None, TILE, d), out_map),
        out_shape=out_shape,
        compiler_params=_cparams(2),
        name="swiglu",
        **kwargs,
    )(xs, modsel, g2, win, wout)


def _rope_tables_t(n_ctx, n_lat):
    n_freq = HEAD_DIM // 4
    pos = jnp.arange(n_lat, dtype=jnp.int32)
    row = (pos // GRID_W).astype(F32)
    col = (pos % GRID_W).astype(F32)
    inv = ROPE_THETA ** (-jnp.arange(n_freq, dtype=F32) / n_freq)
    ang_r = inv[:, None] * row[None, :]
    ang_c = inv[:, None] * col[None, :]
    cos = jnp.concatenate([jnp.cos(ang_r)] * 2 + [jnp.cos(ang_c)] * 2, axis=0)
    sin = jnp.concatenate([-jnp.sin(ang_r), jnp.sin(ang_r), -jnp.sin(ang_c), jnp.sin(ang_c)], axis=0)
    cos = jnp.concatenate([jnp.ones((HEAD_DIM, n_ctx), F32), cos], axis=1)
    sin = jnp.concatenate([jnp.zeros((HEAD_DIM, n_ctx), F32), sin], axis=1)
    return cos, sin


def kernel(x, c, ctx, c_ctx, w_mod, b_mod, g_norm1, w_in, w_gate, b_gate, g_q_gqa, g_k_gqa, g_q_diff, g_k_diff, lambda_q1, lambda_k1, lambda_q2, lambda_k2, g_subln, w_dw, b_dw, g_conv_ln, b_conv_ln, w_a, w_b, w_c, w_out, g_norm2, w_ffn_in, w_ffn_out):
    batch, n_lat, d = x.shape
    n_ctx = ctx.shape[1]
    depth = w_mod.shape[0]
    assert n_ctx == TILE and n_lat % Q_TILE == 0 and d == D_MODEL and batch + 1 <= MOD_ROWS
    n_pad = Q_TILE - n_ctx

    cvec = jnp.concatenate(
        [c, c_ctx[None, :], jnp.zeros((MOD_ROWS - batch - 1, d), F32)], axis=0)
    mods = _modulation(cvec, w_mod, b_mod)
    cos_t, sin_t = _rope_tables_t(n_ctx + n_pad, n_lat)
    xs = jnp.concatenate([ctx, jnp.zeros((batch, n_pad, d), F32), x], axis=1)

    for l in range(depth):
        last = l == depth - 1
        lam_init = 0.8 - 0.6 * math.exp(-0.3 * l)
        lat_mod = mods[l, :batch].reshape(batch, 1, 6, d)
        ctx_mod = jnp.broadcast_to(mods[l, batch].reshape(1, 1, 6, d), (batch, 1, 6, d))
        modsel = jnp.concatenate([ctx_mod, lat_mod], axis=1)
        gains = jnp.broadcast_to(
            jnp.stack([g_q_gqa[l], g_k_gqa[l], g_q_diff[l], g_k_diff[l]])[:, :, None],
            (4, HEAD_DIM, TILE))
        lam_rows = jnp.zeros((8, 128), F32).at[0:4, 0:HEAD_DIM].set(
            jnp.stack([lambda_q1[l], lambda_k1[l], lambda_q2[l], lambda_k2[l]]))
        gsub = jnp.broadcast_to(g_subln[l][:, None], (DIFF_V_DIM, Q_TILE))
        g1 = g_norm1[l].reshape(1, d)
        wt = w_in[l, :, :HEADS_W].T.astype(BF16)
        wcu = w_in[l, :, HEADS_W:].astype(BF16)
        wdw = jnp.concatenate([w_dw[l], jnp.zeros((1, CONV_CH), F32)], axis=0)

        qt, k4, vt4, dqt, dk4, dvt4, u = _project(xs, modsel, g1, wt, wcu, gains, cos_t, sin_t)
        o_gqa = _attention(qt, k4, vt4, g_q_gqa[l], g_k_gqa[l], diff=False, skip_ctx=last)
        o_diff = _attention(dqt, dk4, dvt4, g_q_diff[l], g_k_diff[l], diff=True, skip_ctx=last,
                            extra=(lam_rows, gsub), lam_init=lam_init)
        xs = _merge(xs, modsel, g1, o_gqa, o_diff, u,
                    w_gate[l].astype(BF16), b_gate[l].reshape(1, -1),
                    w_a[l].astype(BF16), w_b[l].astype(BF16), w_c[l].astype(BF16),
                    wdw, b_dw[l].reshape(1, -1), g_conv_ln[l].reshape(1, -1),
                    b_conv_ln[l].reshape(1, -1), w_out[l].astype(BF16), skip_ctx=last)
        xs = _ffn(xs, modsel, g_norm2[l].reshape(1, d), w_ffn_in[l].astype(BF16),
                  w_ffn_out[l].astype(BF16), skip_ctx=last)
    return xs
```

```python
import functools
import math

import jax
import jax.numpy as jnp
from jax import lax
from jax.experimental import pallas as pl
from jax.experimental.pallas import tpu as pltpu

F32 = jnp.float32
BF16 = jnp.bfloat16
F8 = jnp.float8_e4m3fn

D_MODEL = 1024
HEAD_DIM = 64
GQA_Q_HEADS = 8
GQA_KV_HEADS = 2
GQA_GROUP = GQA_Q_HEADS // GQA_KV_HEADS
DIFF_HEADS = 4
DIFF_V_DIM = 2 * HEAD_DIM
CONV_CH = 512
CONV_WIDTH = 31
CONV_HALF = CONV_WIDTH // 2
D_FF = 2816
GRID_W = 64
ROPE_THETA = 10000.0
EPS = 1e-6
ATTN_SCALE = HEAD_DIM ** -0.5
N_BRANCH = 3

GQA_Q_W = GQA_Q_HEADS * HEAD_DIM
GQA_KV_W = GQA_KV_HEADS * HEAD_DIM
DIFF_QK_W = DIFF_HEADS * 2 * HEAD_DIM
DIFF_V_W = DIFF_HEADS * DIFF_V_DIM
CONV_IN_W = 2 * CONV_CH
OFF_GQ = 0
OFF_GK = OFF_GQ + GQA_Q_W
OFF_GV = OFF_GK + GQA_KV_W
OFF_DQ = OFF_GV + GQA_KV_W
OFF_DK = OFF_DQ + DIFF_QK_W
OFF_DV = OFF_DK + DIFF_QK_W
OFF_CU = OFF_DV + DIFF_V_W
D_IN = OFF_CU + CONV_IN_W
HEADS_W = OFF_CU

TILE = 256
Q_TILE = 512
LAT_TILE0 = Q_TILE // TILE
HALO = 16
SUBLANES = 8
SHIFT_ROWS = TILE + 2 * HALO - SUBLANES
MOD_ROWS = 16
MOD_BLOCK_N = 1536
NEG_BIG = -1e30
LOG2E = math.log2(math.e)
Q_SCALE = ATTN_SCALE * LOG2E
QK_SPLIT_W = 4 * HEAD_DIM
F8_TARGET = 256.0
SCORE_BOUND_FAST = 80.0
KEY_BLOCK_CHUNKS = 2
STEPS_PER_TRIP = 4
VMEM_LIMIT = 52 * 1024 * 1024


def _cparams(n_axes):
    return pltpu.CompilerParams(
        dimension_semantics=("arbitrary",) * n_axes, vmem_limit_bytes=VMEM_LIMIT)


def _const_spec(shape):
    nd = len(shape)
    return pl.BlockSpec(shape, lambda *_: (0,) * nd, pipeline_mode=pl.Buffered(1))


def _dot(a, b):
    return jnp.dot(a, b, preferred_element_type=F32)


def _split_bf16(a):
    hi = a.astype(BF16)
    lo = (a - hi.astype(F32)).astype(BF16)
    return hi, lo


def _norm_mod(x, g, shift, scale):
    ms = jnp.mean(x * x, axis=-1, keepdims=True)
    y = x * lax.rsqrt(ms + EPS) * g
    return y * (1.0 + scale) + shift


def _mod_kernel(c_ref, w_ref, b_ref, o_ref):
    c = c_ref[...]
    a = c * jax.nn.sigmoid(c)
    a_hi, a_lo = _split_bf16(a)
    w_hi, w_lo = _split_bf16(w_ref[...])
    o_ref[...] = _dot(a_hi, w_hi) + _dot(a_hi, w_lo) + _dot(a_lo, w_hi) + b_ref[...]


def _modulation(cvec, w_mod, b_mod):
    depth, d, n = w_mod.shape
    return pl.pallas_call(
        _mod_kernel,
        grid=(depth, n // MOD_BLOCK_N),
        in_specs=[
            pl.BlockSpec((MOD_ROWS, d), lambda l, j: (0, 0)),
            pl.BlockSpec((None, d, MOD_BLOCK_N), lambda l, j: (l, 0, j)),
            pl.BlockSpec((None, 1, MOD_BLOCK_N), lambda l, j: (l, 0, j)),
        ],
        out_specs=pl.BlockSpec((None, MOD_ROWS, MOD_BLOCK_N), lambda l, j: (l, 0, j)),
        out_shape=jax.ShapeDtypeStruct((depth, MOD_ROWS, n), F32),
        compiler_params=_cparams(2),
        name="modulation",
    )(cvec, w_mod, b_mod.reshape(depth, 1, n))


def _proj_kernel(x_ref, mod_ref, g1_ref, wt_ref, wcu_ref, gains_ref, cos_ref, sin_ref,
                 qt_ref, k_ref, vt_ref, dqt_ref, dk_ref, dvt_ref, u_ref):
    hb = _norm_mod(x_ref[...], g1_ref[...], mod_ref[0:1, :], mod_ref[1:2, :]).astype(BF16)
    pt = lax.dot_general(wt_ref[...], hb, (((1,), (1,)), ((), ())), preferred_element_type=F32)
    cos = cos_ref[...][None]
    sin = sin_ref[...][None]

    def norm_rope_split(t, gain_idx, is_query):
        n = t.shape[0] // HEAD_DIM
        t3 = t.reshape(n, HEAD_DIM, TILE)
        ms = jnp.mean(t3 * t3, axis=1, keepdims=True)
        y = t3 * lax.rsqrt(ms + EPS) * gains_ref[gain_idx][None]
        q = HEAD_DIM // 4
        swapped = jnp.concatenate(
            [y[:, q:2 * q], y[:, 0:q], y[:, 3 * q:4 * q], y[:, 2 * q:3 * q]], axis=1)
        y = y * cos + swapped * sin
        hi = y.astype(F8).astype(F32)
        lo = y - hi
        zero = jnp.zeros_like(hi)
        parts = [hi, lo, hi, zero] if is_query else [hi, hi, lo, zero]
        return jnp.concatenate(parts, axis=1).reshape(n * QK_SPLIT_W, TILE)

    qt_ref[...] = norm_rope_split(pt[OFF_GQ:OFF_GK], 0, True).astype(F8)
    k_ref[...] = norm_rope_split(pt[OFF_GK:OFF_GV], 1, False).T.astype(F8)
    vt_ref[...] = pt[OFF_GV:OFF_DQ].astype(BF16)
    dqt_ref[...] = norm_rope_split(pt[OFF_DQ:OFF_DK], 2, True).astype(F8)
    dk_ref[...] = norm_rope_split(pt[OFF_DK:OFF_DV], 3, False).T.astype(F8)
    dvt_ref[...] = pt[OFF_DV:OFF_CU].astype(BF16)
    cu = _dot(hb, wcu_ref[...])
    u_ref[...] = cu[:, :CONV_CH] * jax.nn.sigmoid(cu[:, CONV_CH:])


def _project(xs, modsel, g1, wt, wcu, gains, cos_t, sin_t):
    b, t, d = xs.shape
    nt = t // TILE
    tile_map = lambda bi, i: (bi, i, 0)
    chunk_map = lambda bi, i: (bi, i, 0, 0)
    lane_map = lambda bi, i: (bi, 0, i)
    return pl.pallas_call(
        _proj_kernel,
        grid=(b, nt),
        in_specs=[
            pl.BlockSpec((None, TILE, d), tile_map),
            pl.BlockSpec((None, None, 6, d), lambda bi, i: (bi, jnp.minimum(i, 1), 0, 0)),
            _const_spec((1, d)),
            _const_spec((HEADS_W, d)),
            _const_spec((d, CONV_IN_W)),
            _const_spec((4, HEAD_DIM, TILE)),
            pl.BlockSpec((HEAD_DIM, TILE), lambda bi, i: (0, i)),
            pl.BlockSpec((HEAD_DIM, TILE), lambda bi, i: (0, i)),
        ],
        out_specs=[
            pl.BlockSpec((None, 4 * GQA_Q_W, TILE), lane_map),
            pl.BlockSpec((None, None, TILE, 4 * GQA_KV_W), chunk_map),
            pl.BlockSpec((None, None, GQA_KV_W, TILE), chunk_map),
            pl.BlockSpec((None, 4 * DIFF_QK_W, TILE), lane_map),
            pl.BlockSpec((None, None, TILE, 4 * DIFF_QK_W), chunk_map),
            pl.BlockSpec((None, None, DIFF_V_W, TILE), chunk_map),
            pl.BlockSpec((None, TILE, CONV_CH), tile_map),
        ],
        out_shape=[
            jax.ShapeDtypeStruct((b, 4 * GQA_Q_W, t), F8),
            jax.ShapeDtypeStruct((b, nt, TILE, 4 * GQA_KV_W), F8),
            jax.ShapeDtypeStruct((b, nt, GQA_KV_W, TILE), BF16),
            jax.ShapeDtypeStruct((b, 4 * DIFF_QK_W, t), F8),
            jax.ShapeDtypeStruct((b, nt, TILE, 4 * DIFF_QK_W), F8),
            jax.ShapeDtypeStruct((b, nt, DIFF_V_W, TILE), BF16),
            jax.ShapeDtypeStruct((b, t, CONV_CH), F32),
        ],
        compiler_params=_cparams(2),
        name="project",
    )(xs, modsel, g1, wt, wcu, gains, cos_t, sin_t)


def _map_operands(qt_ref, k_chunk, j, diff):
    q = qt_ref[QK_SPLIT_W * j:QK_SPLIT_W * (j + 1), :]
    k = k_chunk[:, QK_SPLIT_W * j:QK_SPLIT_W * (j + 1)] if diff else k_chunk
    return k, q


def _finish_attention(o_ref, outs, diff, lam_ref, gsub_ref, lam_init):
    if diff:
        lam_p = lam_ref[...]
        lam = (jnp.exp(jnp.sum(lam_p[0:1] * lam_p[1:2], axis=1, keepdims=True))
               - jnp.exp(jnp.sum(lam_p[2:3] * lam_p[3:4], axis=1, keepdims=True)) + lam_init)
        o = outs[0] - lam * outs[1]
        ms = jnp.mean(o * o, axis=0, keepdims=True)
        o = o * lax.rsqrt(ms + EPS) * gsub_ref[...] * (1.0 - lam_init)
    else:
        o = jnp.concatenate(outs, axis=0)
    o_ref[...] = o.T.astype(BF16)


def _attn_safe_kernel(*refs, n_maps, diff, q_off, lam_init):
    if diff:
        c_ref, qt_ref, k_ref, vt_ref, lam_ref, gsub_ref, o_ref, m_s, l_s, acc_s = refs
    else:
        c_ref, qt_ref, k_ref, vt_ref, o_ref, m_s, l_s, acc_s = refs
        lam_ref = gsub_ref = None
    q_tile = pl.program_id(2) + q_off
    to_log2 = c_ref[0]
    m_s[...] = jnp.full(m_s.shape, NEG_BIG, F32)
    l_s[...] = jnp.zeros(l_s.shape, F32)
    acc_s[...] = jnp.zeros(acc_s.shape, F32)

    n_chunks = jnp.where(q_tile == 0, 1, k_ref.shape[0] - LAT_TILE0 + 1)

    def body(c, carry):
        chunk = jnp.where(c == 0, 0, c + LAT_TILE0 - 1)
        kc = k_ref[chunk]
        vc = vt_ref[chunk]
        for j in range(n_maps):
            s = _dot(*_map_operands(qt_ref, kc, j, diff)) * to_log2
            m_prev = m_s[j]
            m_new = jnp.maximum(m_prev, jnp.max(s, axis=0, keepdims=True))
            alpha = jnp.exp2(m_prev - m_new)
            p = jnp.exp2(s - m_new)
            l_s[j] = alpha * l_s[j] + jnp.sum(p, axis=0, keepdims=True)
            acc_s[j] = alpha * acc_s[j] + _dot(vc, p.astype(BF16))
            m_s[j] = m_new
        return carry

    lax.fori_loop(0, n_chunks, body, 0)
    _finish_attention(o_ref, [acc_s[j] / l_s[j] for j in range(n_maps)],
                      diff, lam_ref, gsub_ref, lam_init)


def _attn_fast_kernel(*refs, n_maps, dv, diff, q_off, lam_init, key_block):
    if diff:
        c_ref, qt_ref, k_ref, vt_ref, lam_ref, gsub_ref, o_ref, p_a, p_b, acc_s = refs
    else:
        c_ref, qt_ref, k_ref, vt_ref, o_ref, p_a, p_b, acc_s = refs
        lam_ref = gsub_ref = None
    q_tile = pl.program_id(2) + q_off
    chunks_per_block = key_block // TILE
    n_blocks = (k_ref.shape[0] - LAT_TILE0) // chunks_per_block
    to_log2 = c_ref[0]

    def probs(chunk, j):
        s = _dot(*_map_operands(qt_ref, k_ref[chunk], j, diff))
        return jnp.exp2((s * to_log2).astype(BF16))

    def v_ext(chunk):
        v = vt_ref[chunk]
        row = lax.broadcasted_iota(jnp.int32, (16, v.shape[1]), 0)
        return jnp.concatenate([v, jnp.where(row == 0, 1.0, 0.0).astype(BF16)], axis=0)

    ve = v_ext(0)
    for j in range(n_maps):
        acc_s[j] = _dot(ve, probs(0, j))

    def step(src, dst, i, has_next):
        first = LAT_TILE0 + i * chunks_per_block
        for j in range(n_maps):
            acc = acc_s[j]
            for c in range(chunks_per_block):
                rows = slice(c * TILE, (c + 1) * TILE)
                if has_next:
                    dst[j, rows, :] = probs(first + chunks_per_block + c, j)
                acc = acc + _dot(v_ext(first + c), src[j, rows, :])
            acc_s[j] = acc

    def latent_keys():
        for j in range(n_maps):
            for c in range(chunks_per_block):
                p_a[j, c * TILE:(c + 1) * TILE, :] = probs(LAT_TILE0 + c, j)

        def steps(first, n, final):
            for e in range(n):
                src, dst = (p_a, p_b) if e % 2 == 0 else (p_b, p_a)
                step(src, dst, first + e, not (final and e == n - 1))

        def body(t, carry):
            steps(STEPS_PER_TRIP * t, STEPS_PER_TRIP, False)
            return carry

        n_trips = (n_blocks - 1) // STEPS_PER_TRIP
        lax.fori_loop(0, n_trips, body, 0)
        steps(n_trips * STEPS_PER_TRIP, n_blocks - n_trips * STEPS_PER_TRIP, True)

    if q_off == 0:
        pl.when(q_tile != 0)(latent_keys)
    else:
        latent_keys()

    outs = []
    for j in range(n_maps):
        acc = acc_s[j]
        outs.append(acc[0:dv] / acc[dv:dv + 1])
    _finish_attention(o_ref, outs, diff, lam_ref, gsub_ref, lam_init)


def _attention_call(c_scale, qt, k4, vt4, extra, *, diff, skip_ctx, lam_init, fast):
    b, qw, t = qt.shape
    nt = t // TILE
    nq = t // Q_TILE
    if diff:
        n_groups, n_maps, dv = DIFF_HEADS, 2, DIFF_V_DIM
    else:
        n_groups, n_maps, dv = GQA_KV_HEADS, GQA_GROUP, HEAD_DIM
    k_w = k4.shape[-1] // n_groups
    q_off = 1 if skip_ctx else 0
    out_w = qw // QK_SPLIT_W * HEAD_DIM // n_groups
    in_specs = [
        pl.BlockSpec(memory_space=pltpu.SMEM),
        pl.BlockSpec((None, n_maps * QK_SPLIT_W, Q_TILE), lambda bi, g, i: (bi, g, i + q_off)),
        pl.BlockSpec((None, nt, TILE, k_w), lambda bi, g, i: (bi, 0, 0, g),
                     pipeline_mode=pl.Buffered(1)),
        pl.BlockSpec((None, nt, dv, TILE), lambda bi, g, i: (bi, 0, g, 0),
                     pipeline_mode=pl.Buffered(1)),
    ]
    if diff:
        in_specs += [_const_spec((8, 128)), _const_spec((DIFF_V_DIM, Q_TILE))]
    if fast:
        key_block = _key_block(nt - LAT_TILE0)
        body = functools.partial(_attn_fast_kernel, n_maps=n_maps, dv=dv, diff=diff, q_off=q_off,
                                 lam_init=lam_init, key_block=key_block)
        scratch = [pltpu.VMEM((n_maps, key_block, Q_TILE), BF16),
                   pltpu.VMEM((n_maps, key_block, Q_TILE), BF16),
                   pltpu.VMEM((n_maps, dv + 16, Q_TILE), F32)]
    else:
        body = functools.partial(_attn_safe_kernel, n_maps=n_maps, diff=diff, q_off=q_off,
                                 lam_init=lam_init)
        scratch = [pltpu.VMEM((n_maps, 1, Q_TILE), F32),
                   pltpu.VMEM((n_maps, 1, Q_TILE), F32),
                   pltpu.VMEM((n_maps, dv, Q_TILE), F32)]
    kind = ("diff" if diff else "gqa") + ("_attention" if fast else "_attention_safe")
    return pl.pallas_call(
        body,
        grid=(b, n_groups, nq - q_off),
        in_specs=in_specs,
        out_specs=pl.BlockSpec((None, Q_TILE, out_w), lambda bi, g, i: (bi, i + q_off, g)),
        out_shape=jax.ShapeDtypeStruct((b, t, out_w * n_groups), BF16),
        scratch_shapes=scratch,
        compiler_params=_cparams(3),
        name=kind,
    )(c_scale, qt, k4, vt4, *extra)


def _key_block(n_latent_chunks):
    assert n_latent_chunks % 2 == 0
    chunks = KEY_BLOCK_CHUNKS if n_latent_chunks % (2 * KEY_BLOCK_CHUNKS) == 0 else 1
    return chunks * TILE


def _attention(c_scale, qt, k4, vt4, g_q, g_k, *, diff, skip_ctx, extra=(), lam_init=0.0):
    bound = (HEAD_DIM * ATTN_SCALE * LOG2E * (1.0 + 2.0 ** -8) ** 2
             * jnp.max(jnp.abs(g_q)) * jnp.max(jnp.abs(g_k)))
    call = functools.partial(_attention_call, diff=diff, skip_ctx=skip_ctx, lam_init=lam_init)
    return lax.cond(bound <= SCORE_BOUND_FAST,
                    lambda *a: call(*a[:4], a[4:], fast=True),
                    lambda *a: call(*a[:4], a[4:], fast=False),
                    c_scale, qt, k4, vt4, *extra)


def _tile_of(i, skip_ctx):
    if skip_ctx:
        return i + LAT_TILE0
    return jnp.where(i == 0, 0, i + LAT_TILE0 - 1)


def _n_live_tiles(nt, skip_ctx):
    return nt - LAT_TILE0 + (0 if skip_ctx else 1)


def _merge_kernel(x_ref, mod_ref, g1_ref, oa_ref, ob_ref, u_ref, up_ref, un_ref,
                  wgate_ref, bgate_ref, wa_ref, wb_ref, wc_ref, wdw_ref, bdw_ref, gln_ref, bln_ref,
                  wout_ref, o_ref, ubuf, ushift, conv_s, *, skip_ctx, last_tile):
    x = x_ref[...]
    hb = _norm_mod(x, g1_ref[...], mod_ref[0:1, :], mod_ref[1:2, :]).astype(BF16)
    gates = jax.nn.sigmoid(_dot(hb, wgate_ref[...]) + bgate_ref[...])
    y_a = _dot(oa_ref[...], wa_ref[...])
    y_b = _dot(ob_ref[...], wb_ref[...])

    tile = _tile_of(pl.program_id(1), skip_ctx)
    has_prev = jnp.logical_and(tile != 0, tile != LAT_TILE0).astype(F32)
    has_next = jnp.logical_and(tile != 0, tile != last_tile).astype(F32)
    ubuf[0:HALO, :] = up_ref[...] * has_prev
    ubuf[HALO:HALO + TILE, :] = u_ref[...]
    ubuf[HALO + TILE:, :] = un_ref[...] * has_next
    for res in range(1, SUBLANES):
        ushift[res - 1] = ubuf[res:res + SHIFT_ROWS, :]
    rows = 32
    for r in range(0, TILE, rows):
        acc = jnp.zeros((rows, CONV_CH), F32) + bdw_ref[...]
        for j in range(CONV_WIDTH):
            start = HALO - CONV_HALF + j
            res = start % SUBLANES
            base = r + start - res
            window = ubuf[base:base + rows, :] if res == 0 else ushift[res - 1, base:base + rows, :]
            acc = acc + window * wdw_ref[j:j + 1, :]
        conv_s[r:r + rows, :] = acc
    v = conv_s[...]
    mu = jnp.mean(v, axis=-1, keepdims=True)
    vc = v - mu
    var = jnp.mean(vc * vc, axis=-1, keepdims=True)
    v = vc * lax.rsqrt(var + EPS) * gln_ref[...] + bln_ref[...]
    v = v * jax.nn.sigmoid(v)
    y_c = _dot(v.astype(BF16), wc_ref[...])

    merged = (gates[:, 0:D_MODEL] * y_a + gates[:, D_MODEL:2 * D_MODEL] * y_b
              + gates[:, 2 * D_MODEL:] * y_c)
    y = _dot(merged.astype(BF16), wout_ref[...])
    o_ref[...] = x + mod_ref[2:3, :] * y


def _merge(xs, modsel, g1, oa, ob, u, wgate, bgate, wa, wb, wc, wdw, bdw, gln, bln, wout, *, skip_ctx):
    b, t, d = xs.shape
    nt = t // TILE
    hb_per_tile = TILE // HALO
    n_halo_blocks = t // HALO
    tile_of = functools.partial(_tile_of, skip_ctx=skip_ctx)
    tile_map = lambda bi, i: (bi, tile_of(i), 0)
    return pl.pallas_call(
        functools.partial(_merge_kernel, skip_ctx=skip_ctx, last_tile=nt - 1),
        grid=(b, _n_live_tiles(nt, skip_ctx)),
        in_specs=[
            pl.BlockSpec((None, TILE, d), tile_map),
            pl.BlockSpec((None, None, 6, d), lambda bi, i: (bi, jnp.minimum(tile_of(i), 1), 0, 0)),
            _const_spec((1, d)),
            pl.BlockSpec((None, TILE, GQA_Q_W), tile_map),
            pl.BlockSpec((None, TILE, DIFF_V_W), tile_map),
            pl.BlockSpec((None, TILE, CONV_CH), tile_map),
            pl.BlockSpec((None, HALO, CONV_CH),
                         lambda bi, i: (bi, jnp.maximum(tile_of(i) * hb_per_tile - 1, 0), 0)),
            pl.BlockSpec((None, HALO, CONV_CH),
                         lambda bi, i: (bi, jnp.minimum((tile_of(i) + 1) * hb_per_tile,
                                                        n_halo_blocks - 1), 0)),
            _const_spec((d, N_BRANCH * d)),
            _const_spec((1, N_BRANCH * d)),
            _const_spec((GQA_Q_W, d)),
            _const_spec((DIFF_V_W, d)),
            _const_spec((CONV_CH, d)),
            _const_spec((CONV_WIDTH + 1, CONV_CH)),
            _const_spec((1, CONV_CH)),
            _const_spec((1, CONV_CH)),
            _const_spec((1, CONV_CH)),
            _const_spec((d, d)),
        ],
        out_specs=pl.BlockSpec((None, TILE, d), tile_map),
        out_shape=jax.ShapeDtypeStruct(xs.shape, F32),
        scratch_shapes=[
            pltpu.VMEM((TILE + 2 * HALO, CONV_CH), F32),
            pltpu.VMEM((SUBLANES - 1, SHIFT_ROWS, CONV_CH), F32),
            pltpu.VMEM((TILE, CONV_CH), F32),
        ],
        input_output_aliases={0: 0},
        compiler_params=_cparams(2),
        name="merge",
    )(xs, modsel, g1, oa, ob, u, u, u, wgate, bgate, wa, wb, wc, wdw, bdw, gln, bln, wout)


def _ffn_kernel(x_ref, mod_ref, g2_ref, win_ref, wout_ref, o_ref):
    x = x_ref[...]
    hb = _norm_mod(x, g2_ref[...], mod_ref[3:4, :], mod_ref[4:5, :]).astype(BF16)
    gate = _dot(hb, win_ref[:, :D_FF])
    up = _dot(hb, win_ref[:, D_FF:])
    act = (gate * jax.nn.sigmoid(gate) * up).astype(BF16)
    o_ref[...] = x + mod_ref[5:6, :] * _dot(act, wout_ref[...])


def _ffn(xs, modsel, g2, win, wout, *, skip_ctx):
    b, t, d = xs.shape
    nt = t // TILE
    tile_of = functools.partial(_tile_of, skip_ctx=skip_ctx)
    kwargs = {}
    if skip_ctx:
        out_shape = jax.ShapeDtypeStruct((b, t - LAT_TILE0 * TILE, d), F32)
        out_map = lambda bi, i: (bi, i, 0)
    else:
        out_shape = jax.ShapeDtypeStruct(xs.shape, F32)
        out_map = lambda bi, i: (bi, tile_of(i), 0)
        kwargs["input_output_aliases"] = {0: 0}
    return pl.pallas_call(
        _ffn_kernel,
        grid=(b, _n_live_tiles(nt, skip_ctx)),
        in_specs=[
            pl.BlockSpec((None, TILE, d), lambda bi, i: (bi, tile_of(i), 0)),
            pl.BlockSpec((None, None, 6, d), lambda bi, i: (bi, jnp.minimum(tile_of(i), 1), 0, 0)),
            _const_spec((1, d)),
            _const_spec((d, 2 * D_FF)),
            _const_spec((D_FF, d)),
        ],
        out_specs=pl.BlockSpec((None, TILE, d), out_map),
        out_shape=out_shape,
        compiler_params=_cparams(2),
        name="swiglu",
        **kwargs,
    )(xs, modsel, g2, win, wout)


def _rope_tables_t(n_ctx, n_lat):
    n_freq = HEAD_DIM // 4
    pos = jnp.arange(n_lat, dtype=jnp.int32)
    row = (pos // GRID_W).astype(F32)
    col = (pos % GRID_W).astype(F32)
    inv = ROPE_THETA ** (-jnp.arange(n_freq, dtype=F32) / n_freq)
    ang_r = inv[:, None] * row[None, :]
    ang_c = inv[:, None] * col[None, :]
    cos = jnp.concatenate([jnp.cos(ang_r)] * 2 + [jnp.cos(ang_c)] * 2, axis=0)
    sin = jnp.concatenate([-jnp.sin(ang_r), jnp.sin(ang_r), -jnp.sin(ang_c), jnp.sin(ang_c)], axis=0)
    cos = jnp.concatenate([jnp.ones((HEAD_DIM, n_ctx), F32), cos], axis=1)
    sin = jnp.concatenate([jnp.zeros((HEAD_DIM, n_ctx), F32), sin], axis=1)
    return cos, sin


def _f8_prescale(gains):
    peak = jnp.maximum(jnp.max(jnp.abs(gains), axis=-1) * HEAD_DIM ** 0.5, 1e-30)
    return jnp.exp2(jnp.floor(jnp.log2(F8_TARGET / peak)))


def kernel(x, c, ctx, c_ctx, w_mod, b_mod, g_norm1, w_in, w_gate, b_gate, g_q_gqa, g_k_gqa, g_q_diff, g_k_diff, lambda_q1, lambda_k1, lambda_q2, lambda_k2, g_subln, w_dw, b_dw, g_conv_ln, b_conv_ln, w_a, w_b, w_c, w_out, g_norm2, w_ffn_in, w_ffn_out):
    batch, n_lat, d = x.shape
    n_ctx = ctx.shape[1]
    depth = w_mod.shape[0]
    assert n_ctx == TILE and n_lat % Q_TILE == 0 and d == D_MODEL and batch + 1 <= MOD_ROWS
    n_pad = Q_TILE - n_ctx

    cvec = jnp.concatenate(
        [c, c_ctx[None, :], jnp.zeros((MOD_ROWS - batch - 1, d), F32)], axis=0)
    mods = _modulation(cvec, w_mod, b_mod)
    cos_t, sin_t = _rope_tables_t(n_ctx + n_pad, n_lat)
    xs = jnp.concatenate([ctx, jnp.zeros((batch, n_pad, d), F32), x], axis=1)

    for l in range(depth):
        last = l == depth - 1
        lam_init = 0.8 - 0.6 * math.exp(-0.3 * l)
        lat_mod = mods[l, :batch].reshape(batch, 1, 6, d)
        ctx_mod = jnp.broadcast_to(mods[l, batch].reshape(1, 1, 6, d), (batch, 1, 6, d))
        modsel = jnp.concatenate([ctx_mod, lat_mod], axis=1)
        qk_gains = jnp.stack([g_q_gqa[l], g_k_gqa[l], g_q_diff[l], g_k_diff[l]])
        pre = _f8_prescale(qk_gains)
        gains = jnp.broadcast_to((qk_gains * pre[:, None])[:, :, None], (4, HEAD_DIM, TILE))
        c_gqa = (Q_SCALE / (pre[0] * pre[1])).reshape(1)
        c_diff = (Q_SCALE / (pre[2] * pre[3])).reshape(1)
        lam_rows = jnp.zeros((8, 128), F32).at[0:4, 0:HEAD_DIM].set(
            jnp.stack([lambda_q1[l], lambda_k1[l], lambda_q2[l], lambda_k2[l]]))
        gsub = jnp.broadcast_to(g_subln[l][:, None], (DIFF_V_DIM, Q_TILE))
        g1 = g_norm1[l].reshape(1, d)
        wt = w_in[l, :, :HEADS_W].T.astype(BF16)
        wcu = w_in[l, :, HEADS_W:].astype(BF16)
        wdw = jnp.concatenate([w_dw[l], jnp.zeros((1, CONV_CH), F32)], axis=0)

        qt, k4, vt4, dqt, dk4, dvt4, u = _project(xs, modsel, g1, wt, wcu, gains, cos_t, sin_t)
        o_gqa = _attention(c_gqa, qt, k4, vt4, g_q_gqa[l], g_k_gqa[l], diff=False, skip_ctx=last)
        o_diff = _attention(c_diff, dqt, dk4, dvt4, g_q_diff[l], g_k_diff[l], diff=True, skip_ctx=last,
                            extra=(lam_rows, gsub), lam_init=lam_init)
        xs = _merge(xs, modsel, g1, o_gqa, o_diff, u,
                    w_gate[l].astype(BF16), b_gate[l].reshape(1, -1),
                    w_a[l].astype(BF16), w_b[l].astype(BF16), w_c[l].astype(BF16),
                    wdw, b_dw[l].reshape(1, -1), g_conv_ln[l].reshape(1, -1),
                    b_conv_ln[l].reshape(1, -1), w_out[l].astype(BF16), skip_ctx=last)
        xs = _ffn(xs, modsel, g_norm2[l].reshape(1, d), w_ffn_in[l].astype(BF16),
                  w_ffn_out[l].astype(BF16), skip_ctx=last)
    return xs
```

```python
import functools
import math

import jax
import jax.numpy as jnp
from jax import lax
from jax.experimental import pallas as pl
from jax.experimental.pallas import tpu as pltpu

F32 = jnp.float32
BF16 = jnp.bfloat16
F8 = jnp.float8_e4m3fn

D_MODEL = 1024
HEAD_DIM = 64
GQA_Q_HEADS = 8
GQA_KV_HEADS = 2
GQA_GROUP = GQA_Q_HEADS // GQA_KV_HEADS
DIFF_HEADS = 4
DIFF_V_DIM = 2 * HEAD_DIM
CONV_CH = 512
CONV_WIDTH = 31
CONV_HALF = CONV_WIDTH // 2
D_FF = 2816
GRID_W = 64
ROPE_THETA = 10000.0
EPS = 1e-6
ATTN_SCALE = HEAD_DIM ** -0.5
N_BRANCH = 3

GQA_Q_W = GQA_Q_HEADS * HEAD_DIM
GQA_KV_W = GQA_KV_HEADS * HEAD_DIM
DIFF_QK_W = DIFF_HEADS * 2 * HEAD_DIM
DIFF_V_W = DIFF_HEADS * DIFF_V_DIM
CONV_IN_W = 2 * CONV_CH
OFF_GQ = 0
OFF_GK = OFF_GQ + GQA_Q_W
OFF_GV = OFF_GK + GQA_KV_W
OFF_DQ = OFF_GV + GQA_KV_W
OFF_DK = OFF_DQ + DIFF_QK_W
OFF_DV = OFF_DK + DIFF_QK_W
OFF_CU = OFF_DV + DIFF_V_W
D_IN = OFF_CU + CONV_IN_W
HEADS_W = OFF_CU

TILE = 256
Q_TILE = 512
LAT_TILE0 = Q_TILE // TILE
HALO = 16
SUBLANES = 8
SHIFT_ROWS = TILE + 2 * HALO - SUBLANES
MOD_ROWS = 16
MOD_BLOCK_N = 1536
NEG_BIG = -1e30
LOG2E = math.log2(math.e)
Q_SCALE = ATTN_SCALE * LOG2E
QK_SPLIT_W = 4 * HEAD_DIM
F8_TARGET = 256.0
SCORE_BOUND_FAST = 80.0
KEY_BLOCK_CHUNKS = 2
STEPS_PER_TRIP = 4
DIFF_HEADS_PER_STEP = 2
GQA_KV_PER_STEP = 2
VMEM_LIMIT = 52 * 1024 * 1024


def _cparams(n_axes):
    return pltpu.CompilerParams(
        dimension_semantics=("arbitrary",) * n_axes, vmem_limit_bytes=VMEM_LIMIT)


def _const_spec(shape):
    nd = len(shape)
    return pl.BlockSpec(shape, lambda *_: (0,) * nd, pipeline_mode=pl.Buffered(1))


def _dot(a, b):
    return jnp.dot(a, b, preferred_element_type=F32)


def _split_bf16(a):
    hi = a.astype(BF16)
    lo = (a - hi.astype(F32)).astype(BF16)
    return hi, lo


def _norm_mod(x, g, shift, scale):
    ms = jnp.mean(x * x, axis=-1, keepdims=True)
    y = x * lax.rsqrt(ms + EPS) * g
    return y * (1.0 + scale) + shift


def _mod_kernel(c_ref, w_ref, b_ref, o_ref):
    c = c_ref[...]
    a = c * jax.nn.sigmoid(c)
    a_hi, a_lo = _split_bf16(a)
    w_hi, w_lo = _split_bf16(w_ref[...])
    o_ref[...] = _dot(a_hi, w_hi) + _dot(a_hi, w_lo) + _dot(a_lo, w_hi) + b_ref[...]


def _modulation(cvec, w_mod, b_mod):
    depth, d, n = w_mod.shape
    return pl.pallas_call(
        _mod_kernel,
        grid=(depth, n // MOD_BLOCK_N),
        in_specs=[
            pl.BlockSpec((MOD_ROWS, d), lambda l, j: (0, 0)),
            pl.BlockSpec((None, d, MOD_BLOCK_N), lambda l, j: (l, 0, j)),
            pl.BlockSpec((None, 1, MOD_BLOCK_N), lambda l, j: (l, 0, j)),
        ],
        out_specs=pl.BlockSpec((None, MOD_ROWS, MOD_BLOCK_N), lambda l, j: (l, 0, j)),
        out_shape=jax.ShapeDtypeStruct((depth, MOD_ROWS, n), F32),
        compiler_params=_cparams(2),
        name="modulation",
    )(cvec, w_mod, b_mod.reshape(depth, 1, n))


def _proj_kernel(x_ref, mod_ref, g1_ref, wt_ref, wcu_ref, gains_ref, cos_ref, sin_ref,
                 qt_ref, k_ref, vt_ref, dqt_ref, dk_ref, dvt_ref, u_ref):
    hb = _norm_mod(x_ref[...], g1_ref[...], mod_ref[0:1, :], mod_ref[1:2, :]).astype(BF16)
    pt = lax.dot_general(wt_ref[...], hb, (((1,), (1,)), ((), ())), preferred_element_type=F32)
    cos = cos_ref[...][None]
    sin = sin_ref[...][None]

    def norm_rope_split(t, gain_idx, is_query):
        n = t.shape[0] // HEAD_DIM
        t3 = t.reshape(n, HEAD_DIM, TILE)
        ms = jnp.mean(t3 * t3, axis=1, keepdims=True)
        y = t3 * lax.rsqrt(ms + EPS) * gains_ref[gain_idx][None]
        q = HEAD_DIM // 4
        swapped = jnp.concatenate(
            [y[:, q:2 * q], y[:, 0:q], y[:, 3 * q:4 * q], y[:, 2 * q:3 * q]], axis=1)
        y = y * cos + swapped * sin
        hi = y.astype(F8).astype(F32)
        lo = y - hi
        zero = jnp.zeros_like(hi)
        parts = [hi, lo, hi, zero] if is_query else [hi, hi, lo, zero]
        return jnp.concatenate(parts, axis=1).reshape(n * QK_SPLIT_W, TILE)

    qt_ref[...] = norm_rope_split(pt[OFF_GQ:OFF_GK], 0, True).astype(F8)
    k_ref[...] = norm_rope_split(pt[OFF_GK:OFF_GV], 1, False).T.astype(F8)
    vt_ref[...] = pt[OFF_GV:OFF_DQ].astype(BF16)
    dqt_ref[...] = norm_rope_split(pt[OFF_DQ:OFF_DK], 2, True).astype(F8)
    dk_ref[...] = norm_rope_split(pt[OFF_DK:OFF_DV], 3, False).T.astype(F8)
    dvt_ref[...] = pt[OFF_DV:OFF_CU].astype(BF16)
    cu = _dot(hb, wcu_ref[...])
    u_ref[...] = cu[:, :CONV_CH] * jax.nn.sigmoid(cu[:, CONV_CH:])


def _project(xs, modsel, g1, wt, wcu, gains, cos_t, sin_t):
    b, t, d = xs.shape
    nt = t // TILE
    tile_map = lambda bi, i: (bi, i, 0)
    chunk_map = lambda bi, i: (bi, i, 0, 0)
    lane_map = lambda bi, i: (bi, 0, i)
    return pl.pallas_call(
        _proj_kernel,
        grid=(b, nt),
        in_specs=[
            pl.BlockSpec((None, TILE, d), tile_map),
            pl.BlockSpec((None, None, 6, d), lambda bi, i: (bi, jnp.minimum(i, 1), 0, 0)),
            _const_spec((1, d)),
            _const_spec((HEADS_W, d)),
            _const_spec((d, CONV_IN_W)),
            _const_spec((4, HEAD_DIM, TILE)),
            pl.BlockSpec((HEAD_DIM, TILE), lambda bi, i: (0, i)),
            pl.BlockSpec((HEAD_DIM, TILE), lambda bi, i: (0, i)),
        ],
        out_specs=[
            pl.BlockSpec((None, 4 * GQA_Q_W, TILE), lane_map),
            pl.BlockSpec((None, None, TILE, 4 * GQA_KV_W), chunk_map),
            pl.BlockSpec((None, None, GQA_KV_W, TILE), chunk_map),
            pl.BlockSpec((None, 4 * DIFF_QK_W, TILE), lane_map),
            pl.BlockSpec((None, None, TILE, 4 * DIFF_QK_W), chunk_map),
            pl.BlockSpec((None, None, DIFF_V_W, TILE), chunk_map),
            pl.BlockSpec((None, TILE, CONV_CH), tile_map),
        ],
        out_shape=[
            jax.ShapeDtypeStruct((b, 4 * GQA_Q_W, t), F8),
            jax.ShapeDtypeStruct((b, nt, TILE, 4 * GQA_KV_W), F8),
            jax.ShapeDtypeStruct((b, nt, GQA_KV_W, TILE), BF16),
            jax.ShapeDtypeStruct((b, 4 * DIFF_QK_W, t), F8),
            jax.ShapeDtypeStruct((b, nt, TILE, 4 * DIFF_QK_W), F8),
            jax.ShapeDtypeStruct((b, nt, DIFF_V_W, TILE), BF16),
            jax.ShapeDtypeStruct((b, t, CONV_CH), F32),
        ],
        compiler_params=_cparams(2),
        name="project",
    )(xs, modsel, g1, wt, wcu, gains, cos_t, sin_t)


def _map_operands(qt_ref, k_chunk, j, diff):
    q = qt_ref[QK_SPLIT_W * j:QK_SPLIT_W * (j + 1), :]
    kj = j if diff else j // GQA_GROUP
    return k_chunk[:, QK_SPLIT_W * kj:QK_SPLIT_W * (kj + 1)], q


def _map_value(vt_ref, chunk, j, dv, diff):
    head = j // (2 if diff else GQA_GROUP)
    return vt_ref[chunk, dv * head:dv * (head + 1), :]


def _finish_attention(o_ref, outs, diff, lam_ref, gsub_ref, lam_init):
    if diff:
        lam_p = lam_ref[...]
        lam = (jnp.exp(jnp.sum(lam_p[0:1] * lam_p[1:2], axis=1, keepdims=True))
               - jnp.exp(jnp.sum(lam_p[2:3] * lam_p[3:4], axis=1, keepdims=True)) + lam_init)
        heads = []
        for h in range(len(outs) // 2):
            o = outs[2 * h] - lam * outs[2 * h + 1]
            ms = jnp.mean(o * o, axis=0, keepdims=True)
            heads.append(o * lax.rsqrt(ms + EPS) * gsub_ref[...] * (1.0 - lam_init))
        outs = heads
    o_ref[...] = jnp.concatenate(outs, axis=0).T.astype(BF16)


def _attn_safe_kernel(*refs, n_maps, dv, diff, q_off, lam_init):
    if diff:
        c_ref, qt_ref, k_ref, vt_ref, lam_ref, gsub_ref, o_ref, m_s, l_s, acc_s = refs
    else:
        c_ref, qt_ref, k_ref, vt_ref, o_ref, m_s, l_s, acc_s = refs
        lam_ref = gsub_ref = None
    q_tile = pl.program_id(2) + q_off
    to_log2 = c_ref[0]
    m_s[...] = jnp.full(m_s.shape, NEG_BIG, F32)
    l_s[...] = jnp.zeros(l_s.shape, F32)
    acc_s[...] = jnp.zeros(acc_s.shape, F32)

    n_chunks = jnp.where(q_tile == 0, 1, k_ref.shape[0] - LAT_TILE0 + 1)

    def body(c, carry):
        chunk = jnp.where(c == 0, 0, c + LAT_TILE0 - 1)
        kc = k_ref[chunk]
        for j in range(n_maps):
            vc = _map_value(vt_ref, chunk, j, dv, diff)
            s = _dot(*_map_operands(qt_ref, kc, j, diff)) * to_log2
            m_prev = m_s[j]
            m_new = jnp.maximum(m_prev, jnp.max(s, axis=0, keepdims=True))
            alpha = jnp.exp2(m_prev - m_new)
            p = jnp.exp2(s - m_new)
            l_s[j] = alpha * l_s[j] + jnp.sum(p, axis=0, keepdims=True)
            acc_s[j] = alpha * acc_s[j] + _dot(vc, p.astype(BF16))
            m_s[j] = m_new
        return carry

    lax.fori_loop(0, n_chunks, body, 0)
    _finish_attention(o_ref, [acc_s[j] / l_s[j] for j in range(n_maps)],
                      diff, lam_ref, gsub_ref, lam_init)


def _attn_fast_kernel(*refs, n_maps, dv, diff, q_off, lam_init, key_block):
    if diff:
        c_ref, qt_ref, k_ref, vt_ref, lam_ref, gsub_ref, o_ref, p_a, p_b, acc_s = refs
    else:
        c_ref, qt_ref, k_ref, vt_ref, o_ref, p_a, p_b, acc_s = refs
        lam_ref = gsub_ref = None
    q_tile = pl.program_id(2) + q_off
    chunks_per_block = key_block // TILE
    n_blocks = (k_ref.shape[0] - LAT_TILE0) // chunks_per_block
    to_log2 = c_ref[0]

    def probs(chunk, j):
        s = _dot(*_map_operands(qt_ref, k_ref[chunk], j, diff))
        return jnp.exp2((s * to_log2).astype(BF16))

    def v_ext(chunk, j):
        v = _map_value(vt_ref, chunk, j, dv, diff)
        row = lax.broadcasted_iota(jnp.int32, (16, v.shape[1]), 0)
        return jnp.concatenate([v, jnp.where(row == 0, 1.0, 0.0).astype(BF16)], axis=0)

    def context_keys():
        for j in range(n_maps):
            acc_s[j] = _dot(v_ext(0, j), probs(0, j))

    def step(src, dst, i, has_next):
        first = LAT_TILE0 + i * chunks_per_block
        for j in range(n_maps):
            acc = acc_s[j]
            for c in range(chunks_per_block):
                rows = slice(c * TILE, (c + 1) * TILE)
                if has_next:
                    dst[j, rows, :] = probs(first + chunks_per_block + c, j)
                acc = acc + _dot(v_ext(first + c, j), src[j, rows, :])
            acc_s[j] = acc

    def all_keys():
        p_ctx = [probs(0, j) for j in range(n_maps)]
        for j in range(n_maps):
            for c in range(chunks_per_block):
                p_a[j, c * TILE:(c + 1) * TILE, :] = probs(LAT_TILE0 + c, j)
        for j in range(n_maps):
            acc_s[j] = _dot(v_ext(0, j), p_ctx[j])

        def steps(first, n, final):
            for e in range(n):
                src, dst = (p_a, p_b) if e % 2 == 0 else (p_b, p_a)
                step(src, dst, first + e, not (final and e == n - 1))

        def body(t, carry):
            steps(STEPS_PER_TRIP * t, STEPS_PER_TRIP, False)
            return carry

        n_trips = (n_blocks - 1) // STEPS_PER_TRIP
        lax.fori_loop(0, n_trips, body, 0)
        steps(n_trips * STEPS_PER_TRIP, n_blocks - n_trips * STEPS_PER_TRIP, True)

    if q_off == 0:
        pl.when(q_tile == 0)(context_keys)
        pl.when(q_tile != 0)(all_keys)
    else:
        all_keys()

    outs = []
    for j in range(n_maps):
        acc = acc_s[j]
        outs.append(acc[0:dv] / acc[dv:dv + 1])
    _finish_attention(o_ref, outs, diff, lam_ref, gsub_ref, lam_init)


def _attention_call(c_scale, qt, k4, vt4, extra, *, diff, skip_ctx, lam_init, fast):
    b, qw, t = qt.shape
    nt = t // TILE
    nq = t // Q_TILE
    if diff:
        n_groups, n_maps, dv = DIFF_HEADS // DIFF_HEADS_PER_STEP, 2 * DIFF_HEADS_PER_STEP, DIFF_V_DIM
    else:
        n_groups, n_maps, dv = GQA_KV_HEADS // GQA_KV_PER_STEP, GQA_GROUP * GQA_KV_PER_STEP, HEAD_DIM
    k_w = k4.shape[-1] // n_groups
    q_off = 1 if skip_ctx else 0
    out_w = qw // QK_SPLIT_W * HEAD_DIM // n_groups
    in_specs = [
        pl.BlockSpec(memory_space=pltpu.SMEM),
        pl.BlockSpec((None, n_maps * QK_SPLIT_W, Q_TILE), lambda bi, g, i: (bi, g, i + q_off)),
        pl.BlockSpec((None, nt, TILE, k_w), lambda bi, g, i: (bi, 0, 0, g),
                     pipeline_mode=pl.Buffered(1)),
        pl.BlockSpec((None, nt, vt4.shape[2] // n_groups, TILE), lambda bi, g, i: (bi, 0, g, 0),
                     pipeline_mode=pl.Buffered(1)),
    ]
    if diff:
        in_specs += [_const_spec((8, 128)), _const_spec((DIFF_V_DIM, Q_TILE))]
    if fast:
        key_block = _key_block(nt - LAT_TILE0)
        body = functools.partial(_attn_fast_kernel, n_maps=n_maps, dv=dv, diff=diff, q_off=q_off,
                                 lam_init=lam_init, key_block=key_block)
        scratch = [pltpu.VMEM((n_maps, key_block, Q_TILE), BF16),
                   pltpu.VMEM((n_maps, key_block, Q_TILE), BF16),
                   pltpu.VMEM((n_maps, dv + 16, Q_TILE), F32)]
    else:
        body = functools.partial(_attn_safe_kernel, n_maps=n_maps, dv=dv, diff=diff, q_off=q_off,
                                 lam_init=lam_init)
        scratch = [pltpu.VMEM((n_maps, 1, Q_TILE), F32),
                   pltpu.VMEM((n_maps, 1, Q_TILE), F32),
                   pltpu.VMEM((n_maps, dv, Q_TILE), F32)]
    kind = ("diff" if diff else "gqa") + ("_attention" if fast else "_attention_safe")
    return pl.pallas_call(
        body,
        grid=(b, n_groups, nq - q_off),
        in_specs=in_specs,
        out_specs=pl.BlockSpec((None, Q_TILE, out_w), lambda bi, g, i: (bi, i + q_off, g)),
        out_shape=jax.ShapeDtypeStruct((b, t, out_w * n_groups), BF16),
        scratch_shapes=scratch,
        compiler_params=_cparams(3),
        name=kind,
    )(c_scale, qt, k4, vt4, *extra)


def _key_block(n_latent_chunks):
    assert n_latent_chunks % 2 == 0
    chunks = KEY_BLOCK_CHUNKS if n_latent_chunks % (2 * KEY_BLOCK_CHUNKS) == 0 else 1
    return chunks * TILE


def _attention(c_scale, qt, k4, vt4, g_q, g_k, *, diff, skip_ctx, extra=(), lam_init=0.0):
    bound = (HEAD_DIM * ATTN_SCALE * LOG2E * (1.0 + 2.0 ** -8) ** 2
             * jnp.max(jnp.abs(g_q)) * jnp.max(jnp.abs(g_k)))
    call = functools.partial(_attention_call, diff=diff, skip_ctx=skip_ctx, lam_init=lam_init)
    return lax.cond(bound <= SCORE_BOUND_FAST,
                    lambda *a: call(*a[:4], a[4:], fast=True),
                    lambda *a: call(*a[:4], a[4:], fast=False),
                    c_scale, qt, k4, vt4, *extra)


def _tile_of(i, skip_ctx):
    if skip_ctx:
        return i + LAT_TILE0
    return jnp.where(i == 0, 0, i + LAT_TILE0 - 1)


def _n_live_tiles(nt, skip_ctx):
    return nt - LAT_TILE0 + (0 if skip_ctx else 1)


def _merge_kernel(x_ref, mod_ref, g1_ref, oa_ref, ob_ref, u_ref, up_ref, un_ref,
                  wgate_ref, bgate_ref, wa_ref, wb_ref, wc_ref, wdw_ref, bdw_ref, gln_ref, bln_ref,
                  wout_ref, o_ref, ubuf, ushift, conv_s, *, skip_ctx, last_tile):
    x = x_ref[...]
    hb = _norm_mod(x, g1_ref[...], mod_ref[0:1, :], mod_ref[1:2, :]).astype(BF16)
    gates = jax.nn.sigmoid(_dot(hb, wgate_ref[...]) + bgate_ref[...])
    y_a = _dot(oa_ref[...], wa_ref[...])
    y_b = _dot(ob_ref[...], wb_ref[...])

    tile = _tile_of(pl.program_id(1), skip_ctx)
    has_prev = jnp.logical_and(tile != 0, tile != LAT_TILE0).astype(F32)
    has_next = jnp.logical_and(tile != 0, tile != last_tile).astype(F32)
    ubuf[0:HALO, :] = up_ref[...] * has_prev
    ubuf[HALO:HALO + TILE, :] = u_ref[...]
    ubuf[HALO + TILE:, :] = un_ref[...] * has_next
    for res in range(1, SUBLANES):
        ushift[res - 1] = ubuf[res:res + SHIFT_ROWS, :]
    rows = 32
    for r in range(0, TILE, rows):
        acc = jnp.zeros((rows, CONV_CH), F32) + bdw_ref[...]
        for j in range(CONV_WIDTH):
            start = HALO - CONV_HALF + j
            res = start % SUBLANES
            base = r + start - res
            window = ubuf[base:base + rows, :] if res == 0 else ushift[res - 1, base:base + rows, :]
            acc = acc + window * wdw_ref[j:j + 1, :]
        conv_s[r:r + rows, :] = acc
    v = conv_s[...]
    mu = jnp.mean(v, axis=-1, keepdims=True)
    vc = v - mu
    var = jnp.mean(vc * vc, axis=-1, keepdims=True)
    v = vc * lax.rsqrt(var + EPS) * gln_ref[...] + bln_ref[...]
    v = v * jax.nn.sigmoid(v)
    y_c = _dot(v.astype(BF16), wc_ref[...])

    merged = (gates[:, 0:D_MODEL] * y_a + gates[:, D_MODEL:2 * D_MODEL] * y_b
              + gates[:, 2 * D_MODEL:] * y_c)
    y = _dot(merged.astype(BF16), wout_ref[...])
    o_ref[...] = x + mod_ref[2:3, :] * y


def _merge(xs, modsel, g1, oa, ob, u, wgate, bgate, wa, wb, wc, wdw, bdw, gln, bln, wout, *, skip_ctx):
    b, t, d = xs.shape
    nt = t // TILE
    hb_per_tile = TILE // HALO
    n_halo_blocks = t // HALO
    tile_of = functools.partial(_tile_of, skip_ctx=skip_ctx)
    tile_map = lambda bi, i: (bi, tile_of(i), 0)
    return pl.pallas_call(
        functools.partial(_merge_kernel, skip_ctx=skip_ctx, last_tile=nt - 1),
        grid=(b, _n_live_tiles(nt, skip_ctx)),
        in_specs=[
            pl.BlockSpec((None, TILE, d), tile_map),
            pl.BlockSpec((None, None, 6, d), lambda bi, i: (bi, jnp.minimum(tile_of(i), 1), 0, 0)),
            _const_spec((1, d)),
            pl.BlockSpec((None, TILE, GQA_Q_W), tile_map),
            pl.BlockSpec((None, TILE, DIFF_V_W), tile_map),
            pl.BlockSpec((None, TILE, CONV_CH), tile_map),
            pl.BlockSpec((None, HALO, CONV_CH),
                         lambda bi, i: (bi, jnp.maximum(tile_of(i) * hb_per_tile - 1, 0), 0)),
            pl.BlockSpec((None, HALO, CONV_CH),
                         lambda bi, i: (bi, jnp.minimum((tile_of(i) + 1) * hb_per_tile,
                                                        n_halo_blocks - 1), 0)),
            _const_spec((d, N_BRANCH * d)),
            _const_spec((1, N_BRANCH * d)),
            _const_spec((GQA_Q_W, d)),
            _const_spec((DIFF_V_W, d)),
            _const_spec((CONV_CH, d)),
            _const_spec((CONV_WIDTH + 1, CONV_CH)),
            _const_spec((1, CONV_CH)),
            _const_spec((1, CONV_CH)),
            _const_spec((1, CONV_CH)),
            _const_spec((d, d)),
        ],
        out_specs=pl.BlockSpec((None, TILE, d), tile_map),
        out_shape=jax.ShapeDtypeStruct(xs.shape, F32),
        scratch_shapes=[
            pltpu.VMEM((TILE + 2 * HALO, CONV_CH), F32),
            pltpu.VMEM((SUBLANES - 1, SHIFT_ROWS, CONV_CH), F32),
            pltpu.VMEM((TILE, CONV_CH), F32),
        ],
        input_output_aliases={0: 0},
        compiler_params=_cparams(2),
        name="merge",
    )(xs, modsel, g1, oa, ob, u, u, u, wgate, bgate, wa, wb, wc, wdw, bdw, gln, bln, wout)


def _ffn_kernel(x_ref, mod_ref, g2_ref, win_ref, wout_ref, o_ref):
    x = x_ref[...]
    hb = _norm_mod(x, g2_ref[...], mod_ref[3:4, :], mod_ref[4:5, :]).astype(BF16)
    gate = _dot(hb, win_ref[:, :D_FF])
    up = _dot(hb, win_ref[:, D_FF:])
    act = (gate * jax.nn.sigmoid(gate) * up).astype(BF16)
    o_ref[...] = x + mod_ref[5:6, :] * _dot(act, wout_ref[...])


def _ffn(xs, modsel, g2, win, wout, *, skip_ctx):
    b, t, d = xs.shape
    nt = t // TILE
    tile_of = functools.partial(_tile_of, skip_ctx=skip_ctx)
    kwargs = {}
    if skip_ctx:
        out_shape = jax.ShapeDtypeStruct((b, t - LAT_TILE0 * TILE, d), F32)
        out_map = lambda bi, i: (bi, i, 0)
    else:
        out_shape = jax.ShapeDtypeStruct(xs.shape, F32)
        out_map = lambda bi, i: (bi, tile_of(i), 0)
        kwargs["input_output_aliases"] = {0: 0}
    return pl.pallas_call(
        _ffn_kernel,
        grid=(b, _n_live_tiles(nt, skip_ctx)),
        in_specs=[
            pl.BlockSpec((None, TILE, d), lambda bi, i: (bi, tile_of(i), 0)),
            pl.BlockSpec((None, None, 6, d), lambda bi, i: (bi, jnp.minimum(tile_of(i), 1), 0, 0)),
            _const_spec((1, d)),
            _const_spec((d, 2 * D_FF)),
            _const_spec((D_FF, d)),
        ],
        out_specs=pl.BlockSpec((None, TILE, d), out_map),
        out_shape=out_shape,
        compiler_params=_cparams(2),
        name="swiglu",
        **kwargs,
    )(xs, modsel, g2, win, wout)


def _rope_tables_t(n_ctx, n_lat):
    n_freq = HEAD_DIM // 4
    pos = jnp.arange(n_lat, dtype=jnp.int32)
    row = (pos // GRID_W).astype(F32)
    col = (pos % GRID_W).astype(F32)
    inv = ROPE_THETA ** (-jnp.arange(n_freq, dtype=F32) / n_freq)
    ang_r = inv[:, None] * row[None, :]
    ang_c = inv[:, None] * col[None, :]
    cos = jnp.concatenate([jnp.cos(ang_r)] * 2 + [jnp.cos(ang_c)] * 2, axis=0)
    sin = jnp.concatenate([-jnp.sin(ang_r), jnp.sin(ang_r), -jnp.sin(ang_c), jnp.sin(ang_c)], axis=0)
    cos = jnp.concatenate([jnp.ones((HEAD_DIM, n_ctx), F32), cos], axis=1)
    sin = jnp.concatenate([jnp.zeros((HEAD_DIM, n_ctx), F32), sin], axis=1)
    return cos, sin


def _f8_prescale(gains):
    peak = jnp.maximum(jnp.max(jnp.abs(gains), axis=-1) * HEAD_DIM ** 0.5, 1e-30)
    return jnp.exp2(jnp.floor(jnp.log2(F8_TARGET / peak)))


def kernel(x, c, ctx, c_ctx, w_mod, b_mod, g_norm1, w_in, w_gate, b_gate, g_q_gqa, g_k_gqa, g_q_diff, g_k_diff, lambda_q1, lambda_k1, lambda_q2, lambda_k2, g_subln, w_dw, b_dw, g_conv_ln, b_conv_ln, w_a, w_b, w_c, w_out, g_norm2, w_ffn_in, w_ffn_out):
    batch, n_lat, d = x.shape
    n_ctx = ctx.shape[1]
    depth = w_mod.shape[0]
    assert n_ctx == TILE and n_lat % Q_TILE == 0 and d == D_MODEL and batch + 1 <= MOD_ROWS
    n_pad = Q_TILE - n_ctx

    cvec = jnp.concatenate(
        [c, c_ctx[None, :], jnp.zeros((MOD_ROWS - batch - 1, d), F32)], axis=0)
    mods = _modulation(cvec, w_mod, b_mod)
    cos_t, sin_t = _rope_tables_t(n_ctx + n_pad, n_lat)
    xs = jnp.concatenate([ctx, jnp.zeros((batch, n_pad, d), F32), x], axis=1)

    for l in range(depth):
        last = l == depth - 1
        lam_init = 0.8 - 0.6 * math.exp(-0.3 * l)
        lat_mod = mods[l, :batch].reshape(batch, 1, 6, d)
        ctx_mod = jnp.broadcast_to(mods[l, batch].reshape(1, 1, 6, d), (batch, 1, 6, d))
        modsel = jnp.concatenate([ctx_mod, lat_mod], axis=1)
        qk_gains = jnp.stack([g_q_gqa[l], g_k_gqa[l], g_q_diff[l], g_k_diff[l]])
        pre = _f8_prescale(qk_gains)
        gains = jnp.broadcast_to((qk_gains * pre[:, None])[:, :, None], (4, HEAD_DIM, TILE))
        c_gqa = (Q_SCALE / (pre[0] * pre[1])).reshape(1)
        c_diff = (Q_SCALE / (pre[2] * pre[3])).reshape(1)
        lam_rows = jnp.zeros((8, 128), F32).at[0:4, 0:HEAD_DIM].set(
            jnp.stack([lambda_q1[l], lambda_k1[l], lambda_q2[l], lambda_k2[l]]))
        gsub = jnp.broadcast_to(g_subln[l][:, None], (DIFF_V_DIM, Q_TILE))
        g1 = g_norm1[l].reshape(1, d)
        wt = w_in[l, :, :HEADS_W].T.astype(BF16)
        wcu = w_in[l, :, HEADS_W:].astype(BF16)
        wdw = jnp.concatenate([w_dw[l], jnp.zeros((1, CONV_CH), F32)], axis=0)

        qt, k4, vt4, dqt, dk4, dvt4, u = _project(xs, modsel, g1, wt, wcu, gains, cos_t, sin_t)
        o_gqa = _attention(c_gqa, qt, k4, vt4, g_q_gqa[l], g_k_gqa[l], diff=False, skip_ctx=last)
        o_diff = _attention(c_diff, dqt, dk4, dvt4, g_q_diff[l], g_k_diff[l], diff=True, skip_ctx=last,
                            extra=(lam_rows, gsub), lam_init=lam_init)
        xs = _merge(xs, modsel, g1, o_gqa, o_diff, u,
                    w_gate[l].astype(BF16), b_gate[l].reshape(1, -1),
                    w_a[l].astype(BF16), w_b[l].astype(BF16), w_c[l].astype(BF16),
                    wdw, b_dw[l].reshape(1, -1), g_conv_ln[l].reshape(1, -1),
                    b_conv_ln[l].reshape(1, -1), w_out[l].astype(BF16), skip_ctx=last)
        xs = _ffn(xs, modsel, g_norm2[l].reshape(1, d), w_ffn_in[l].astype(BF16),
                  w_ffn_out[l].astype(BF16), skip_ctx=last)
    return xs
```

```python
import functools
import math

import jax
import jax.numpy as jnp
from jax import lax
from jax.experimental import pallas as pl
from jax.experimental.pallas import tpu as pltpu

F32 = jnp.float32
BF16 = jnp.bfloat16
F8 = jnp.float8_e4m3fn

D_MODEL = 1024
HEAD_DIM = 64
GQA_Q_HEADS = 8
GQA_KV_HEADS = 2
GQA_GROUP = GQA_Q_HEADS // GQA_KV_HEADS
DIFF_HEADS = 4
DIFF_V_DIM = 2 * HEAD_DIM
CONV_CH = 512
CONV_WIDTH = 31
CONV_HALF = CONV_WIDTH // 2
D_FF = 2816
GRID_W = 64
ROPE_THETA = 10000.0
EPS = 1e-6
ATTN_SCALE = HEAD_DIM ** -0.5
N_BRANCH = 3

GQA_Q_W = GQA_Q_HEADS * HEAD_DIM
GQA_KV_W = GQA_KV_HEADS * HEAD_DIM
DIFF_QK_W = DIFF_HEADS * 2 * HEAD_DIM
DIFF_V_W = DIFF_HEADS * DIFF_V_DIM
CONV_IN_W = 2 * CONV_CH
OFF_GQ = 0
OFF_GK = OFF_GQ + GQA_Q_W
OFF_GV = OFF_GK + GQA_KV_W
OFF_DQ = OFF_GV + GQA_KV_W
OFF_DK = OFF_DQ + DIFF_QK_W
OFF_DV = OFF_DK + DIFF_QK_W
OFF_CU = OFF_DV + DIFF_V_W
D_IN = OFF_CU + CONV_IN_W
HEADS_W = OFF_CU

TILE = 256
Q_TILE = 512
LAT_TILE0 = Q_TILE // TILE
HALO = 16
SUBLANES = 8
SHIFT_ROWS = TILE + 2 * HALO - SUBLANES
MOD_ROWS = 16
MOD_BLOCK_N = 1536
NEG_BIG = -1e30
LOG2E = math.log2(math.e)
Q_SCALE = ATTN_SCALE * LOG2E
QK_SPLIT_W = 4 * HEAD_DIM
F8_TARGET = 256.0
SCORE_BOUND_FAST = 80.0
KEY_BLOCK_CHUNKS = 2
STEPS_PER_TRIP = 8
DIFF_HEADS_PER_STEP = 2
GQA_KV_PER_STEP = 2
VMEM_LIMIT = 52 * 1024 * 1024


def _cparams(n_axes):
    return pltpu.CompilerParams(
        dimension_semantics=("arbitrary",) * n_axes, vmem_limit_bytes=VMEM_LIMIT)


def _const_spec(shape):
    nd = len(shape)
    return pl.BlockSpec(shape, lambda *_: (0,) * nd, pipeline_mode=pl.Buffered(1))


def _dot(a, b):
    return jnp.dot(a, b, preferred_element_type=F32)


def _split_bf16(a):
    hi = a.astype(BF16)
    lo = (a - hi.astype(F32)).astype(BF16)
    return hi, lo


def _norm_mod(x, g, shift, scale):
    ms = jnp.mean(x * x, axis=-1, keepdims=True)
    y = x * lax.rsqrt(ms + EPS) * g
    return y * (1.0 + scale) + shift


def _mod_kernel(c_ref, w_ref, b_ref, o_ref):
    c = c_ref[...]
    a = c * jax.nn.sigmoid(c)
    a_hi, a_lo = _split_bf16(a)
    w_hi, w_lo = _split_bf16(w_ref[...])
    o_ref[...] = _dot(a_hi, w_hi) + _dot(a_hi, w_lo) + _dot(a_lo, w_hi) + b_ref[...]


def _modulation(cvec, w_mod, b_mod):
    depth, d, n = w_mod.shape
    return pl.pallas_call(
        _mod_kernel,
        grid=(depth, n // MOD_BLOCK_N),
        in_specs=[
            pl.BlockSpec((MOD_ROWS, d), lambda l, j: (0, 0)),
            pl.BlockSpec((None, d, MOD_BLOCK_N), lambda l, j: (l, 0, j)),
            pl.BlockSpec((None, 1, MOD_BLOCK_N), lambda l, j: (l, 0, j)),
        ],
        out_specs=pl.BlockSpec((None, MOD_ROWS, MOD_BLOCK_N), lambda l, j: (l, 0, j)),
        out_shape=jax.ShapeDtypeStruct((depth, MOD_ROWS, n), F32),
        compiler_params=_cparams(2),
        name="modulation",
    )(cvec, w_mod, b_mod.reshape(depth, 1, n))


def _proj_kernel(x_ref, mod_ref, g1_ref, wt_ref, wcu_ref, gains_ref, cos_ref, sin_ref,
                 qt_ref, k_ref, vt_ref, dqt_ref, dk_ref, dvt_ref, u_ref):
    hb = _norm_mod(x_ref[...], g1_ref[...], mod_ref[0:1, :], mod_ref[1:2, :]).astype(BF16)
    pt = lax.dot_general(wt_ref[...], hb, (((1,), (1,)), ((), ())), preferred_element_type=F32)
    cos = cos_ref[...][None]
    sin = sin_ref[...][None]

    def norm_rope_split(t, gain_idx, is_query):
        n = t.shape[0] // HEAD_DIM
        t3 = t.reshape(n, HEAD_DIM, TILE)
        ms = jnp.mean(t3 * t3, axis=1, keepdims=True)
        y = t3 * lax.rsqrt(ms + EPS) * gains_ref[gain_idx][None]
        q = HEAD_DIM // 4
        swapped = jnp.concatenate(
            [y[:, q:2 * q], y[:, 0:q], y[:, 3 * q:4 * q], y[:, 2 * q:3 * q]], axis=1)
        y = y * cos + swapped * sin
        hi = y.astype(F8).astype(F32)
        lo = y - hi
        zero = jnp.zeros_like(hi)
        parts = [hi, lo, hi, zero] if is_query else [hi, hi, lo, zero]
        return jnp.concatenate(parts, axis=1).reshape(n * QK_SPLIT_W, TILE)

    qt_ref[...] = norm_rope_split(pt[OFF_GQ:OFF_GK], 0, True).astype(F8)
    k_ref[...] = norm_rope_split(pt[OFF_GK:OFF_GV], 1, False).T.astype(F8)
    vt_ref[...] = pt[OFF_GV:OFF_DQ].astype(BF16)
    dqt_ref[...] = norm_rope_split(pt[OFF_DQ:OFF_DK], 2, True).astype(F8)
    dk_ref[...] = norm_rope_split(pt[OFF_DK:OFF_DV], 3, False).T.astype(F8)
    dvt_ref[...] = pt[OFF_DV:OFF_CU].astype(BF16)
    cu = _dot(hb, wcu_ref[...])
    u_ref[...] = cu[:, :CONV_CH] * jax.nn.sigmoid(cu[:, CONV_CH:])


def _project(xs, modsel, g1, wt, wcu, gains, cos_t, sin_t):
    b, t, d = xs.shape
    nt = t // TILE
    tile_map = lambda bi, i: (bi, i, 0)
    chunk_map = lambda bi, i: (bi, i, 0, 0)
    lane_map = lambda bi, i: (bi, 0, i)
    return pl.pallas_call(
        _proj_kernel,
        grid=(b, nt),
        in_specs=[
            pl.BlockSpec((None, TILE, d), tile_map),
            pl.BlockSpec((None, None, 6, d), lambda bi, i: (bi, jnp.minimum(i, 1), 0, 0)),
            _const_spec((1, d)),
            _const_spec((HEADS_W, d)),
            _const_spec((d, CONV_IN_W)),
            _const_spec((4, HEAD_DIM, TILE)),
            pl.BlockSpec((HEAD_DIM, TILE), lambda bi, i: (0, i)),
            pl.BlockSpec((HEAD_DIM, TILE), lambda bi, i: (0, i)),
        ],
        out_specs=[
            pl.BlockSpec((None, 4 * GQA_Q_W, TILE), lane_map),
            pl.BlockSpec((None, None, TILE, 4 * GQA_KV_W), chunk_map),
            pl.BlockSpec((None, None, GQA_KV_W, TILE), chunk_map),
            pl.BlockSpec((None, 4 * DIFF_QK_W, TILE), lane_map),
            pl.BlockSpec((None, None, TILE, 4 * DIFF_QK_W), chunk_map),
            pl.BlockSpec((None, None, DIFF_V_W, TILE), chunk_map),
            pl.BlockSpec((None, TILE, CONV_CH), tile_map),
        ],
        out_shape=[
            jax.ShapeDtypeStruct((b, 4 * GQA_Q_W, t), F8),
            jax.ShapeDtypeStruct((b, nt, TILE, 4 * GQA_KV_W), F8),
            jax.ShapeDtypeStruct((b, nt, GQA_KV_W, TILE), BF16),
            jax.ShapeDtypeStruct((b, 4 * DIFF_QK_W, t), F8),
            jax.ShapeDtypeStruct((b, nt, TILE, 4 * DIFF_QK_W), F8),
            jax.ShapeDtypeStruct((b, nt, DIFF_V_W, TILE), BF16),
            jax.ShapeDtypeStruct((b, t, CONV_CH), F32),
        ],
        compiler_params=_cparams(2),
        name="project",
    )(xs, modsel, g1, wt, wcu, gains, cos_t, sin_t)


def _map_operands(qt_ref, k_chunk, j, diff):
    q = qt_ref[QK_SPLIT_W * j:QK_SPLIT_W * (j + 1), :]
    kj = j if diff else j // GQA_GROUP
    return k_chunk[:, QK_SPLIT_W * kj:QK_SPLIT_W * (kj + 1)], q


def _map_value(vt_ref, chunk, j, dv, diff):
    head = j // (2 if diff else GQA_GROUP)
    return vt_ref[chunk, dv * head:dv * (head + 1), :]


def _finish_attention(o_ref, outs, diff, lam_ref, gsub_ref, lam_init):
    if diff:
        lam_p = lam_ref[...]
        lam = (jnp.exp(jnp.sum(lam_p[0:1] * lam_p[1:2], axis=1, keepdims=True))
               - jnp.exp(jnp.sum(lam_p[2:3] * lam_p[3:4], axis=1, keepdims=True)) + lam_init)
        heads = []
        for h in range(len(outs) // 2):
            o = outs[2 * h] - lam * outs[2 * h + 1]
            ms = jnp.mean(o * o, axis=0, keepdims=True)
            heads.append(o * lax.rsqrt(ms + EPS) * gsub_ref[...] * (1.0 - lam_init))
        outs = heads
    o_ref[...] = jnp.concatenate(outs, axis=0).T.astype(BF16)


def _attn_safe_kernel(*refs, n_maps, dv, diff, q_off, lam_init):
    if diff:
        c_ref, qt_ref, k_ref, vt_ref, lam_ref, gsub_ref, o_ref, m_s, l_s, acc_s = refs
    else:
        c_ref, qt_ref, k_ref, vt_ref, o_ref, m_s, l_s, acc_s = refs
        lam_ref = gsub_ref = None
    q_tile = pl.program_id(2) + q_off
    to_log2 = c_ref[0]
    m_s[...] = jnp.full(m_s.shape, NEG_BIG, F32)
    l_s[...] = jnp.zeros(l_s.shape, F32)
    acc_s[...] = jnp.zeros(acc_s.shape, F32)

    n_chunks = jnp.where(q_tile == 0, 1, k_ref.shape[0] - LAT_TILE0 + 1)

    def body(c, carry):
        chunk = jnp.where(c == 0, 0, c + LAT_TILE0 - 1)
        kc = k_ref[chunk]
        for j in range(n_maps):
            vc = _map_value(vt_ref, chunk, j, dv, diff)
            s = _dot(*_map_operands(qt_ref, kc, j, diff)) * to_log2
            m_prev = m_s[j]
            m_new = jnp.maximum(m_prev, jnp.max(s, axis=0, keepdims=True))
            alpha = jnp.exp2(m_prev - m_new)
            p = jnp.exp2(s - m_new)
            l_s[j] = alpha * l_s[j] + jnp.sum(p, axis=0, keepdims=True)
            acc_s[j] = alpha * acc_s[j] + _dot(vc, p.astype(BF16))
            m_s[j] = m_new
        return carry

    lax.fori_loop(0, n_chunks, body, 0)
    _finish_attention(o_ref, [acc_s[j] / l_s[j] for j in range(n_maps)],
                      diff, lam_ref, gsub_ref, lam_init)


def _attn_fast_kernel(*refs, n_maps, dv, diff, q_off, lam_init, key_block):
    if diff:
        c_ref, qt_ref, k_ref, vt_ref, lam_ref, gsub_ref, o_ref, p_a, p_b, acc_s = refs
    else:
        c_ref, qt_ref, k_ref, vt_ref, o_ref, p_a, p_b, acc_s = refs
        lam_ref = gsub_ref = None
    q_tile = pl.program_id(2) + q_off
    chunks_per_block = key_block // TILE
    n_blocks = (k_ref.shape[0] - LAT_TILE0) // chunks_per_block
    to_log2 = c_ref[0]

    def probs(chunk, j):
        s = _dot(*_map_operands(qt_ref, k_ref[chunk], j, diff))
        return jnp.exp2((s * to_log2).astype(BF16))

    def v_ext(chunk, j):
        v = _map_value(vt_ref, chunk, j, dv, diff)
        row = lax.broadcasted_iota(jnp.int32, (16, v.shape[1]), 0)
        return jnp.concatenate([v, jnp.where(row == 0, 1.0, 0.0).astype(BF16)], axis=0)

    def context_keys():
        for j in range(n_maps):
            acc_s[j] = _dot(v_ext(0, j), probs(0, j))

    def step(src, dst, i, has_next):
        first = LAT_TILE0 + i * chunks_per_block
        for j in range(n_maps):
            acc = acc_s[j]
            for c in range(chunks_per_block):
                rows = slice(c * TILE, (c + 1) * TILE)
                if has_next:
                    dst[j, rows, :] = probs(first + chunks_per_block + c, j)
                acc = acc + _dot(v_ext(first + c, j), src[j, rows, :])
            acc_s[j] = acc

    def all_keys():
        p_ctx = [probs(0, j) for j in range(n_maps)]
        for j in range(n_maps):
            for c in range(chunks_per_block):
                p_a[j, c * TILE:(c + 1) * TILE, :] = probs(LAT_TILE0 + c, j)
        for j in range(n_maps):
            acc_s[j] = _dot(v_ext(0, j), p_ctx[j])

        def steps(first, n, final):
            for e in range(n):
                src, dst = (p_a, p_b) if e % 2 == 0 else (p_b, p_a)
                step(src, dst, first + e, not (final and e == n - 1))

        def body(t, carry):
            steps(STEPS_PER_TRIP * t, STEPS_PER_TRIP, False)
            return carry

        n_trips = (n_blocks - 1) // STEPS_PER_TRIP
        lax.fori_loop(0, n_trips, body, 0)
        steps(n_trips * STEPS_PER_TRIP, n_blocks - n_trips * STEPS_PER_TRIP, True)

    if q_off == 0:
        pl.when(q_tile == 0)(context_keys)
        pl.when(q_tile != 0)(all_keys)
    else:
        all_keys()

    outs = []
    for j in range(n_maps):
        acc = acc_s[j]
        outs.append(acc[0:dv] / acc[dv:dv + 1])
    _finish_attention(o_ref, outs, diff, lam_ref, gsub_ref, lam_init)


def _attention_call(c_scale, qt, k4, vt4, extra, *, diff, skip_ctx, lam_init, fast):
    b, qw, t = qt.shape
    nt = t // TILE
    nq = t // Q_TILE
    if diff:
        n_groups, n_maps, dv = DIFF_HEADS // DIFF_HEADS_PER_STEP, 2 * DIFF_HEADS_PER_STEP, DIFF_V_DIM
    else:
        n_groups, n_maps, dv = GQA_KV_HEADS // GQA_KV_PER_STEP, GQA_GROUP * GQA_KV_PER_STEP, HEAD_DIM
    k_w = k4.shape[-1] // n_groups
    q_off = 1 if skip_ctx else 0
    out_w = qw // QK_SPLIT_W * HEAD_DIM // n_groups
    in_specs = [
        pl.BlockSpec(memory_space=pltpu.SMEM),
        pl.BlockSpec((None, n_maps * QK_SPLIT_W, Q_TILE), lambda bi, g, i: (bi, g, i + q_off)),
        pl.BlockSpec((None, nt, TILE, k_w), lambda bi, g, i: (bi, 0, 0, g),
                     pipeline_mode=pl.Buffered(1)),
        pl.BlockSpec((None, nt, vt4.shape[2] // n_groups, TILE), lambda bi, g, i: (bi, 0, g, 0),
                     pipeline_mode=pl.Buffered(1)),
    ]
    if diff:
        in_specs += [_const_spec((8, 128)), _const_spec((DIFF_V_DIM, Q_TILE))]
    if fast:
        key_block = _key_block(nt - LAT_TILE0)
        body = functools.partial(_attn_fast_kernel, n_maps=n_maps, dv=dv, diff=diff, q_off=q_off,
                                 lam_init=lam_init, key_block=key_block)
        scratch = [pltpu.VMEM((n_maps, key_block, Q_TILE), BF16),
                   pltpu.VMEM((n_maps, key_block, Q_TILE), BF16),
                   pltpu.VMEM((n_maps, dv + 16, Q_TILE), F32)]
    else:
        body = functools.partial(_attn_safe_kernel, n_maps=n_maps, dv=dv, diff=diff, q_off=q_off,
                                 lam_init=lam_init)
        scratch = [pltpu.VMEM((n_maps, 1, Q_TILE), F32),
                   pltpu.VMEM((n_maps, 1, Q_TILE), F32),
                   pltpu.VMEM((n_maps, dv, Q_TILE), F32)]
    kind = ("diff" if diff else "gqa") + ("_attention" if fast else "_attention_safe")
    return pl.pallas_call(
        body,
        grid=(b, n_groups, nq - q_off),
        in_specs=in_specs,
        out_specs=pl.BlockSpec((None, Q_TILE, out_w), lambda bi, g, i: (bi, i + q_off, g)),
        out_shape=jax.ShapeDtypeStruct((b, t, out_w * n_groups), BF16),
        scratch_shapes=scratch,
        compiler_params=_cparams(3),
        name=kind,
    )(c_scale, qt, k4, vt4, *extra)


def _key_block(n_latent_chunks):
    assert n_latent_chunks % 2 == 0
    chunks = KEY_BLOCK_CHUNKS if n_latent_chunks % (2 * KEY_BLOCK_CHUNKS) == 0 else 1
    return chunks * TILE


def _attention(c_scale, qt, k4, vt4, g_q, g_k, *, diff, skip_ctx, extra=(), lam_init=0.0):
    bound = (HEAD_DIM * ATTN_SCALE * LOG2E * (1.0 + 2.0 ** -8) ** 2
             * jnp.max(jnp.abs(g_q)) * jnp.max(jnp.abs(g_k)))
    call = functools.partial(_attention_call, diff=diff, skip_ctx=skip_ctx, lam_init=lam_init)
    return lax.cond(bound <= SCORE_BOUND_FAST,
                    lambda *a: call(*a[:4], a[4:], fast=True),
                    lambda *a: call(*a[:4], a[4:], fast=False),
                    c_scale, qt, k4, vt4, *extra)


def _tile_of(i, skip_ctx):
    if skip_ctx:
        return i + LAT_TILE0
    return jnp.where(i == 0, 0, i + LAT_TILE0 - 1)


def _n_live_tiles(nt, skip_ctx):
    return nt - LAT_TILE0 + (0 if skip_ctx else 1)


def _merge_kernel(x_ref, mod_ref, g1_ref, oa_ref, ob_ref, u_ref, up_ref, un_ref,
                  wgate_ref, bgate_ref, wa_ref, wb_ref, wc_ref, wdw_ref, bdw_ref, gln_ref, bln_ref,
                  wout_ref, o_ref, ubuf, ushift, conv_s, *, skip_ctx, last_tile):
    x = x_ref[...]
    hb = _norm_mod(x, g1_ref[...], mod_ref[0:1, :], mod_ref[1:2, :]).astype(BF16)
    gates = jax.nn.sigmoid(_dot(hb, wgate_ref[...]) + bgate_ref[...])
    y_a = _dot(oa_ref[...], wa_ref[...])
    y_b = _dot(ob_ref[...], wb_ref[...])

    tile = _tile_of(pl.program_id(1), skip_ctx)
    has_prev = jnp.logical_and(tile != 0, tile != LAT_TILE0).astype(F32)
    has_next = jnp.logical_and(tile != 0, tile != last_tile).astype(F32)
    ubuf[0:HALO, :] = up_ref[...] * has_prev
    ubuf[HALO:HALO + TILE, :] = u_ref[...]
    ubuf[HALO + TILE:, :] = un_ref[...] * has_next
    for res in range(1, SUBLANES):
        ushift[res - 1] = ubuf[res:res + SHIFT_ROWS, :]
    rows = 32
    for r in range(0, TILE, rows):
        acc = jnp.zeros((rows, CONV_CH), F32) + bdw_ref[...]
        for j in range(CONV_WIDTH):
            start = HALO - CONV_HALF + j
            res = start % SUBLANES
            base = r + start - res
            window = ubuf[base:base + rows, :] if res == 0 else ushift[res - 1, base:base + rows, :]
            acc = acc + window * wdw_ref[j:j + 1, :]
        conv_s[r:r + rows, :] = acc
    v = conv_s[...]
    mu = jnp.mean(v, axis=-1, keepdims=True)
    vc = v - mu
    var = jnp.mean(vc * vc, axis=-1, keepdims=True)
    v = vc * lax.rsqrt(var + EPS) * gln_ref[...] + bln_ref[...]
    v = v * jax.nn.sigmoid(v)
    y_c = _dot(v.astype(BF16), wc_ref[...])

    merged = (gates[:, 0:D_MODEL] * y_a + gates[:, D_MODEL:2 * D_MODEL] * y_b
              + gates[:, 2 * D_MODEL:] * y_c)
    y = _dot(merged.astype(BF16), wout_ref[...])
    o_ref[...] = x + mod_ref[2:3, :] * y


def _merge(xs, modsel, g1, oa, ob, u, wgate, bgate, wa, wb, wc, wdw, bdw, gln, bln, wout, *, skip_ctx):
    b, t, d = xs.shape
    nt = t // TILE
    hb_per_tile = TILE // HALO
    n_halo_blocks = t // HALO
    tile_of = functools.partial(_tile_of, skip_ctx=skip_ctx)
    tile_map = lambda bi, i: (bi, tile_of(i), 0)
    return pl.pallas_call(
        functools.partial(_merge_kernel, skip_ctx=skip_ctx, last_tile=nt - 1),
        grid=(b, _n_live_tiles(nt, skip_ctx)),
        in_specs=[
            pl.BlockSpec((None, TILE, d), tile_map),
            pl.BlockSpec((None, None, 6, d), lambda bi, i: (bi, jnp.minimum(tile_of(i), 1), 0, 0)),
            _const_spec((1, d)),
            pl.BlockSpec((None, TILE, GQA_Q_W), tile_map),
            pl.BlockSpec((None, TILE, DIFF_V_W), tile_map),
            pl.BlockSpec((None, TILE, CONV_CH), tile_map),
            pl.BlockSpec((None, HALO, CONV_CH),
                         lambda bi, i: (bi, jnp.maximum(tile_of(i) * hb_per_tile - 1, 0), 0)),
            pl.BlockSpec((None, HALO, CONV_CH),
                         lambda bi, i: (bi, jnp.minimum((tile_of(i) + 1) * hb_per_tile,
                                                        n_halo_blocks - 1), 0)),
            _const_spec((d, N_BRANCH * d)),
            _const_spec((1, N_BRANCH * d)),
            _const_spec((GQA_Q_W, d)),
            _const_spec((DIFF_V_W, d)),
            _const_spec((CONV_CH, d)),
            _const_spec((CONV_WIDTH + 1, CONV_CH)),
            _const_spec((1, CONV_CH)),
            _const_spec((1, CONV_CH)),
            _const_spec((1, CONV_CH)),
            _const_spec((d, d)),
        ],
        out_specs=pl.BlockSpec((None, TILE, d), tile_map),
        out_shape=jax.ShapeDtypeStruct(xs.shape, F32),
        scratch_shapes=[
            pltpu.VMEM((TILE + 2 * HALO, CONV_CH), F32),
            pltpu.VMEM((SUBLANES - 1, SHIFT_ROWS, CONV_CH), F32),
            pltpu.VMEM((TILE, CONV_CH), F32),
        ],
        input_output_aliases={0: 0},
        compiler_params=_cparams(2),
        name="merge",
    )(xs, modsel, g1, oa, ob, u, u, u, wgate, bgate, wa, wb, wc, wdw, bdw, gln, bln, wout)


def _ffn_kernel(x_ref, mod_ref, g2_ref, win_ref, wout_ref, o_ref):
    x = x_ref[...]
    hb = _norm_mod(x, g2_ref[...], mod_ref[3:4, :], mod_ref[4:5, :]).astype(BF16)
    gate = _dot(hb, win_ref[:, :D_FF])
    up = _dot(hb, win_ref[:, D_FF:])
    act = (gate * jax.nn.sigmoid(gate) * up).astype(BF16)
    o_ref[...] = x + mod_ref[5:6, :] * _dot(act, wout_ref[...])


def _ffn(xs, modsel, g2, win, wout, *, skip_ctx):
    b, t, d = xs.shape
    nt = t // TILE
    tile_of = functools.partial(_tile_of, skip_ctx=skip_ctx)
    kwargs = {}
    if skip_ctx:
        out_shape = jax.ShapeDtypeStruct((b, t - LAT_TILE0 * TILE, d), F32)
        out_map = lambda bi, i: (bi, i, 0)
    else:
        out_shape = jax.ShapeDtypeStruct(xs.shape, F32)
        out_map = lambda bi, i: (bi, tile_of(i), 0)
        kwargs["input_output_aliases"] = {0: 0}
    return pl.pallas_call(
        _ffn_kernel,
        grid=(b, _n_live_tiles(nt, skip_ctx)),
        in_specs=[
            pl.BlockSpec((None, TILE, d), lambda bi, i: (bi, tile_of(i), 0)),
            pl.BlockSpec((None, None, 6, d), lambda bi, i: (bi, jnp.minimum(tile_of(i), 1), 0, 0)),
            _const_spec((1, d)),
            _const_spec((d, 2 * D_FF)),
            _const_spec((D_FF, d)),
        ],
        out_specs=pl.BlockSpec((None, TILE, d), out_map),
        out_shape=out_shape,
        compiler_params=_cparams(2),
        name="swiglu",
        **kwargs,
    )(xs, modsel, g2, win, wout)


def _rope_tables_t(n_ctx, n_lat):
    n_freq = HEAD_DIM // 4
    pos = jnp.arange(n_lat, dtype=jnp.int32)
    row = (pos // GRID_W).astype(F32)
    col = (pos % GRID_W).astype(F32)
    inv = ROPE_THETA ** (-jnp.arange(n_freq, dtype=F32) / n_freq)
    ang_r = inv[:, None] * row[None, :]
    ang_c = inv[:, None] * col[None, :]
    cos = jnp.concatenate([jnp.cos(ang_r)] * 2 + [jnp.cos(ang_c)] * 2, axis=0)
    sin = jnp.concatenate([-jnp.sin(ang_r), jnp.sin(ang_r), -jnp.sin(ang_c), jnp.sin(ang_c)], axis=0)
    cos = jnp.concatenate([jnp.ones((HEAD_DIM, n_ctx), F32), cos], axis=1)
    sin = jnp.concatenate([jnp.zeros((HEAD_DIM, n_ctx), F32), sin], axis=1)
    return cos, sin


def _f8_prescale(gains):
    peak = jnp.maximum(jnp.max(jnp.abs(gains), axis=-1) * HEAD_DIM ** 0.5, 1e-30)
    return jnp.exp2(jnp.floor(jnp.log2(F8_TARGET / peak)))


def kernel(x, c, ctx, c_ctx, w_mod, b_mod, g_norm1, w_in, w_gate, b_gate, g_q_gqa, g_k_gqa, g_q_diff, g_k_diff, lambda_q1, lambda_k1, lambda_q2, lambda_k2, g_subln, w_dw, b_dw, g_conv_ln, b_conv_ln, w_a, w_b, w_c, w_out, g_norm2, w_ffn_in, w_ffn_out):
    batch, n_lat, d = x.shape
    n_ctx = ctx.shape[1]
    depth = w_mod.shape[0]
    assert n_ctx == TILE and n_lat % Q_TILE == 0 and d == D_MODEL and batch + 1 <= MOD_ROWS
    n_pad = Q_TILE - n_ctx

    cvec = jnp.concatenate(
        [c, c_ctx[None, :], jnp.zeros((MOD_ROWS - batch - 1, d), F32)], axis=0)
    mods = _modulation(cvec, w_mod, b_mod)
    cos_t, sin_t = _rope_tables_t(n_ctx + n_pad, n_lat)
    xs = jnp.concatenate([ctx, jnp.zeros((batch, n_pad, d), F32), x], axis=1)

    for l in range(depth):
        last = l == depth - 1
        lam_init = 0.8 - 0.6 * math.exp(-0.3 * l)
        lat_mod = mods[l, :batch].reshape(batch, 1, 6, d)
        ctx_mod = jnp.broadcast_to(mods[l, batch].reshape(1, 1, 6, d), (batch, 1, 6, d))
        modsel = jnp.concatenate([ctx_mod, lat_mod], axis=1)
        qk_gains = jnp.stack([g_q_gqa[l], g_k_gqa[l], g_q_diff[l], g_k_diff[l]])
        pre = _f8_prescale(qk_gains)
        gains = jnp.broadcast_to((qk_gains * pre[:, None])[:, :, None], (4, HEAD_DIM, TILE))
        c_gqa = (Q_SCALE / (pre[0] * pre[1])).reshape(1)
        c_diff = (Q_SCALE / (pre[2] * pre[3])).reshape(1)
        lam_rows = jnp.zeros((8, 128), F32).at[0:4, 0:HEAD_DIM].set(
            jnp.stack([lambda_q1[l], lambda_k1[l], lambda_q2[l], lambda_k2[l]]))
        gsub = jnp.broadcast_to(g_subln[l][:, None], (DIFF_V_DIM, Q_TILE))
        g1 = g_norm1[l].reshape(1, d)
        wt = w_in[l, :, :HEADS_W].T.astype(BF16)
        wcu = w_in[l, :, HEADS_W:].astype(BF16)
        wdw = jnp.concatenate([w_dw[l], jnp.zeros((1, CONV_CH), F32)], axis=0)

        qt, k4, vt4, dqt, dk4, dvt4, u = _project(xs, modsel, g1, wt, wcu, gains, cos_t, sin_t)
        o_gqa = _attention(c_gqa, qt, k4, vt4, g_q_gqa[l], g_k_gqa[l], diff=False, skip_ctx=last)
        o_diff = _attention(c_diff, dqt, dk4, dvt4, g_q_diff[l], g_k_diff[l], diff=True, skip_ctx=last,
                            extra=(lam_rows, gsub), lam_init=lam_init)
        xs = _merge(xs, modsel, g1, o_gqa, o_diff, u,
                    w_gate[l].astype(BF16), b_gate[l].reshape(1, -1),
                    w_a[l].astype(BF16), w_b[l].astype(BF16), w_c[l].astype(BF16),
                    wdw, b_dw[l].reshape(1, -1), g_conv_ln[l].reshape(1, -1),
                    b_conv_ln[l].reshape(1, -1), w_out[l].astype(BF16), skip_ctx=last)
        xs = _ffn(xs, modsel, g_norm2[l].reshape(1, d), w_ffn_in[l].astype(BF16),
                  w_ffn_out[l].astype(BF16), skip_ctx=last)
    return xs
```
